```python
import math
import jax, jax.numpy as jnp
from jax import lax
import numpy as np

D_MODEL = 1024
BATCH = 2
SEQ = 16384
DEPTH = 2

CHUNK = 64
P_DIM = 256
EPS = 1e-6
DN_HEADS = 4
DN_DK = 128
DN_DV = 128
DN_CONV = 4
GLA_HEADS = 4
GLA_DK = 64
GLA_DV = 128
GLA_RANK = 16
GLA_NORMALIZER = 16.0
S5_WIDTH = 512
S5_GROUP = 16
S5_GROUPS = S5_WIDTH // S5_GROUP
S5_STATE = 64
N_BRANCH = 3
BRANCH_WIDTH = 512
D_FF = 2816
FFN_CONV = 3

DN_QK = DN_HEADS * DN_DK
DN_V = DN_HEADS * DN_DV
GLA_QK = GLA_HEADS * GLA_DK
GLA_V = GLA_HEADS * GLA_DV
IN_SPLITS = (DN_QK, DN_QK, DN_V, DN_HEADS, DN_HEADS, DN_V,
             GLA_QK, GLA_QK, GLA_V, GLA_RANK, GLA_V, S5_WIDTH)
D_IN = sum(IN_SPLITS)

kernel_name = "hybrid_deltanet_gla_s5_gated_merge"


def rmsnorm(x, g):
    xf = x.astype(jnp.float32)
    y = xf * lax.rsqrt(jnp.mean(xf * xf, axis=-1, keepdims=True) + EPS)
    return (y * g.astype(jnp.float32)).astype(x.dtype)


def l2norm(x):
    return x * lax.rsqrt(jnp.sum(x * x, axis=-1, keepdims=True) + EPS)


def split_columns(z, sizes):
    offsets = np.cumsum(sizes)[:-1].tolist()
    return jnp.split(z, offsets, axis=-1)


def causal_dwconv(x, w):
    K = w.shape[0]
    L = x.shape[1]
    xp = jnp.pad(x, ((0, 0), (K - 1, 0), (0, 0)))
    return sum(w[k] * xp[:, k:k + L] for k in range(K))


def to_chunks(t):
    Bsz, L = t.shape[:2]
    return jnp.moveaxis(t.reshape(Bsz, L // CHUNK, CHUNK, *t.shape[2:]), 2, 3)


def gated_delta_rule(q, k, v, beta, g_log):
    Bsz, L, H, dk = q.shape
    dv = v.shape[-1]
    q, k, v, beta, g_log = map(to_chunks, (q, k, v, beta, g_log))
    gam = jnp.cumsum(g_log, axis=-1)
    diff = gam[..., :, None] - gam[..., None, :]
    incl = jnp.tril(jnp.ones((CHUNK, CHUNK), bool))
    strict = jnp.tril(jnp.ones((CHUNK, CHUNK), bool), -1)
    dec_incl = jnp.exp(jnp.where(incl, diff, -jnp.inf))
    dec_strict = jnp.where(strict, dec_incl, 0.0)
    kk = jnp.einsum('bnhtd,bnhsd->bnhts', k, k)
    tri = jnp.eye(CHUNK, dtype=q.dtype) + beta[..., :, None] * kk * dec_strict
    rhs = jnp.concatenate([beta[..., None] * v, (beta * jnp.exp(gam))[..., None] * k], axis=-1)
    sol = lax.linalg.triangular_solve(tri, rhs, left_side=True, lower=True, unit_diagonal=True)
    u_new, w = sol[..., :dv], sol[..., dv:]
    attn = jnp.einsum('bnhtd,bnhsd->bnhts', q, k) * dec_incl
    q_dec = q * jnp.exp(gam)[..., None]
    k_dec = k * jnp.exp(gam[..., -1:] - gam)[..., None]
    g_end = jnp.exp(gam[..., -1])

    def step(S, inp):
        u_c, w_c, a_c, qd, kd, ge = inp
        u = u_c - jnp.einsum('bhck,bhkv->bhcv', w_c, S)
        o = jnp.einsum('bhck,bhkv->bhcv', qd, S) + jnp.einsum('bhts,bhsv->bhtv', a_c, u)
        S = ge[..., None, None] * S + jnp.einsum('bhck,bhcv->bhkv', kd, u)
        return S, o

    xs = tuple(jnp.moveaxis(t, 1, 0) for t in (u_new, w, attn, q_dec, k_dec, g_end))
    S0 = jnp.zeros((Bsz, H, dk, dv), q.dtype)
    _, o = lax.scan(step, S0, xs)
    return o.transpose(1, 0, 3, 2, 4).reshape(Bsz, L, H, dv)


def gla_rule(q, k, v, g_log):
    Bsz, L, H, dk = q.shape
    dv = v.shape[-1]
    q, k, v, g_log = map(to_chunks, (q, k, v, g_log))
    b = jnp.cumsum(g_log, axis=3)
    q_e = q * jnp.exp(b)
    k_e = k * jnp.exp(-b)
    incl = jnp.tril(jnp.ones((CHUNK, CHUNK), bool))
    attn = jnp.where(incl, jnp.einsum('bnhtd,bnhsd->bnhts', q_e, k_e), 0.0)
    intra = jnp.einsum('bnhts,bnhsv->bnhtv', attn, v)
    k_dec = k * jnp.exp(b[..., -1:, :] - b)
    dS = jnp.einsum('bnhck,bnhcv->bnhkv', k_dec, v)
    g_end = jnp.exp(b[..., -1, :])

    def step(S, inp):
        qe, ds, ge = inp
        o = jnp.einsum('bhck,bhkv->bhcv', qe, S)
        return ge[..., None] * S + ds, o

    xs = tuple(jnp.moveaxis(t, 1, 0) for t in (q_e, dS, g_end))
    S0 = jnp.zeros((Bsz, H, dk, dv), q.dtype)
    _, inter = lax.scan(step, S0, xs)
    o = intra + jnp.moveaxis(inter, 0, 1)
    return o.transpose(0, 1, 3, 2, 4).reshape(Bsz, L, H, dv)


def deltanet_branch(q, k, v, b_raw, a_raw, gate, conv_w, a_log, dt_bias, norm_g):
    Bsz, L, _ = q.shape
    f32 = jnp.float32
    qkv = jax.nn.silu(causal_dwconv(jnp.concatenate([q, k, v], axis=-1), conv_w))
    q, k, v = split_columns(qkv.astype(f32), (DN_QK, DN_QK, DN_V))
    q = l2norm(q.reshape(Bsz, L, DN_HEADS, DN_DK)) * DN_DK ** -0.5
    k = l2norm(k.reshape(Bsz, L, DN_HEADS, DN_DK))
    v = v.reshape(Bsz, L, DN_HEADS, DN_DV)
    beta = jax.nn.sigmoid(b_raw.astype(f32))
    g_log = -jnp.exp(a_log.astype(f32)) * jax.nn.softplus(a_raw.astype(f32) + dt_bias.astype(f32))
    o = gated_delta_rule(q, k, v, beta, g_log)
    o = rmsnorm(o, norm_g) * jax.nn.silu(gate.astype(f32).reshape(Bsz, L, DN_HEADS, DN_DV))
    return o.reshape(Bsz, L, DN_V).astype(gate.dtype)


def gla_branch(q, k, v, lr, r, w2, b2, norm_g):
    Bsz, L, _ = q.shape
    f32 = jnp.float32
    q = q.astype(f32).reshape(Bsz, L, GLA_HEADS, GLA_DK) * GLA_DK ** -0.5
    k = k.astype(f32).reshape(Bsz, L, GLA_HEADS, GLA_DK)
    v = v.astype(f32).reshape(Bsz, L, GLA_HEADS, GLA_DV)
    g_log = jax.nn.log_sigmoid((lr @ w2 + b2).astype(f32)) / GLA_NORMALIZER
    g_log = g_log.reshape(Bsz, L, GLA_HEADS, GLA_DK)
    o = gla_rule(q, k, v, g_log)
    o = rmsnorm(o, norm_g) * jax.nn.silu(r.astype(f32).reshape(Bsz, L, GLA_HEADS, GLA_DV))
    return o.reshape(Bsz, L, GLA_V).astype(r.dtype)


def s5_branch(u, lam_re, lam_im, log_step, b_re, b_im, c_re, c_im, d, w_glu, b_glu):
    Bsz, L, _ = u.shape
    f32 = jnp.float32
    uf = u.astype(f32)
    lr, li = lam_re.astype(f32), lam_im.astype(f32)
    step = jnp.exp(log_step.astype(f32))[:, None]
    mag = jnp.exp(lr * step)
    abar_re, abar_im = mag * jnp.cos(li * step), mag * jnp.sin(li * step)
    den = lr * lr + li * li
    nr, ni = abar_re - 1.0, abar_im
    fr = (nr * lr + ni * li) / den
    fi = (ni * lr - nr * li) / den
    br, bi = b_re.astype(f32), b_im.astype(f32)
    bbar_re = fr[..., None] * br - fi[..., None] * bi
    bbar_im = fr[..., None] * bi + fi[..., None] * br
    ug = uf.reshape(Bsz, L, S5_GROUPS, S5_GROUP)
    bu_re = jnp.einsum('blgc,gnc->blgn', ug, bbar_re)
    bu_im = jnp.einsum('blgc,gnc->blgn', ug, bbar_im)
    a_re = jnp.broadcast_to(abar_re, (1, L, S5_GROUPS, S5_STATE))
    a_im = jnp.broadcast_to(abar_im, (1, L, S5_GROUPS, S5_STATE))

    def combine(e1, e2):
        a1r, a1i, b1r, b1i = e1
        a2r, a2i, b2r, b2i = e2
        return (a1r * a2r - a1i * a2i, a1r * a2i + a1i * a2r,
                a2r * b1r - a2i * b1i + b2r, a2r * b1i + a2i * b1r + b2i)

    _, _, h_re, h_im = lax.associative_scan(combine, (a_re, a_im, bu_re, bu_im), axis=1)
    y = (jnp.einsum('blgn,gcn->blgc', h_re, c_re.astype(f32))
         - jnp.einsum('blgn,gcn->blgc', h_im, c_im.astype(f32)))
    y = y.reshape(Bsz, L, S5_WIDTH) + d.astype(f32) * uf
    y = jax.nn.gelu(y)
    y = y * jax.nn.sigmoid(y @ w_glu.astype(f32) + b_glu.astype(f32))
    return y.astype(u.dtype)


def token_mixer(h, w_in, dn_conv_w, dn_a_log, dn_dt_bias, dn_norm, gla_w2, gla_b2, gla_norm,
                s5_lam_re, s5_lam_im, s5_log_step, s5_b_re, s5_b_im, s5_c_re, s5_c_im, s5_d,
                s5_w_glu, s5_b_glu, w_gate, b_gate, w_branch, w_o):
    Bsz, L, _ = h.shape
    z = h @ w_in
    (dn_q, dn_k, dn_v, dn_b, dn_a, dn_g,
     gl_q, gl_k, gl_v, gl_lr, gl_r, s5_u) = split_columns(z, IN_SPLITS)
    y_a = deltanet_branch(dn_q, dn_k, dn_v, dn_b, dn_a, dn_g, dn_conv_w, dn_a_log, dn_dt_bias, dn_norm)
    y_b = gla_branch(gl_q, gl_k, gl_v, gl_lr, gl_r, gla_w2, gla_b2, gla_norm)
    y_c = s5_branch(s5_u, s5_lam_re, s5_lam_im, s5_log_step, s5_b_re, s5_b_im,
                    s5_c_re, s5_c_im, s5_d, s5_w_glu, s5_b_glu)
    br = jnp.stack([y_a, y_b, y_c], axis=2).astype(h.dtype)
    proj = jnp.einsum('blgc,gcd->blgd', br, w_branch)
    gates = jax.nn.sigmoid((h @ w_gate + b_gate).reshape(Bsz, L, N_BRANCH, D_MODEL))
    merged = jnp.sum(gates * proj, axis=2)
    return merged @ w_o


def conv_glu(h, w_up, conv_w, conv_b, w_down):
    g, u = jnp.split(h @ w_up, 2, axis=-1)
    g = causal_dwconv(g, conv_w) + conv_b
    return (jax.nn.gelu(g) * u) @ w_down


def setup_inputs(seed: int = 0) -> dict:
    key = jax.random.key(seed)
    ks = iter(jax.random.split(key, 48))
    f32 = jnp.float32

    def nrm(shape, scale):
        return scale * jax.random.normal(next(ks), shape, f32)

    def gain(shape):
        return 1.0 + 0.02 * jax.random.normal(next(ks), shape, f32)

    G, N = S5_GROUPS, S5_STATE
    dt = jnp.exp(jax.random.uniform(next(ks), (DEPTH, DN_HEADS), f32, math.log(1e-3), math.log(1e-1)))
    n_idx = jnp.arange(N, dtype=f32)
    return {
        "x": nrm((BATCH, SEQ, D_MODEL), 1.0),
        "p": nrm((DEPTH, BATCH, SEQ, P_DIM), 1.0),
        "attn_norm": gain((DEPTH, D_MODEL)),
        "w_in": nrm((DEPTH, D_MODEL, D_IN), D_MODEL ** -0.5),
        "dn_conv_w": nrm((DEPTH, DN_CONV, DN_QK * 2 + DN_V), DN_CONV ** -0.5),
        "dn_a_log": jnp.log(jax.random.uniform(next(ks), (DEPTH, DN_HEADS), f32, 1.0, 16.0)),
        "dn_dt_bias": dt + jnp.log(-jnp.expm1(-dt)),
        "dn_norm": gain((DEPTH, DN_DV)),
        "gla_w2": nrm((DEPTH, GLA_RANK, GLA_QK), GLA_RANK ** -0.5),
        "gla_b2": nrm((DEPTH, GLA_QK), 0.1),
        "gla_norm": gain((DEPTH, GLA_DV)),
        "s5_lam_re": -0.5 + nrm((DEPTH, G, N), 0.01),
        "s5_lam_im": math.pi * n_idx + nrm((DEPTH, G, N), 0.01),
        "s5_log_step": jax.random.uniform(next(ks), (DEPTH, G), f32, math.log(1e-3), math.log(1e-1)),
        "s5_b_re": nrm((DEPTH, G, N, S5_GROUP), (2 * S5_GROUP) ** -0.5),
        "s5_b_im": nrm((DEPTH, G, N, S5_GROUP), (2 * S5_GROUP) ** -0.5),
        "s5_c_re": nrm((DEPTH, G, S5_GROUP, N), N ** -0.5),
        "s5_c_im": nrm((DEPTH, G, S5_GROUP, N), N ** -0.5),
        "s5_d": nrm((DEPTH, S5_WIDTH), 1.0),
        "s5_w_glu": nrm((DEPTH, S5_WIDTH, S5_WIDTH), S5_WIDTH ** -0.5),
        "s5_b_glu": nrm((DEPTH, S5_WIDTH), 0.02),
        "w_gate": nrm((DEPTH, D_MODEL, N_BRANCH * D_MODEL), D_MODEL ** -0.5),
        "b_gate": nrm((DEPTH, N_BRANCH * D_MODEL), 0.02),
        "w_branch": nrm((DEPTH, N_BRANCH, BRANCH_WIDTH, D_MODEL), BRANCH_WIDTH ** -0.5),
        "w_o": nrm((DEPTH, D_MODEL, D_MODEL), D_MODEL ** -0.5),
        "ffn_norm": gain((DEPTH, D_MODEL)),
        "w_up": nrm((DEPTH, D_MODEL, 2 * D_FF), D_MODEL ** -0.5),
        "ffn_conv_w": nrm((DEPTH, FFN_CONV, D_FF), FFN_CONV ** -0.5),
        "ffn_conv_b": nrm((DEPTH, D_FF), 0.02),
        "w_down": nrm((DEPTH, D_FF, D_MODEL), D_FF ** -0.5),
        "ple_norm": gain((DEPTH, D_MODEL)),
        "w_ple_gate": nrm((DEPTH, D_MODEL, D_MODEL), D_MODEL ** -0.5),
        "w_ple_proj": nrm((DEPTH, P_DIM, D_MODEL), P_DIM ** -0.5),
        "final_norm": gain((D_MODEL,)),
    }


def reference(x, p, attn_norm, w_in, dn_conv_w, dn_a_log, dn_dt_bias, dn_norm, gla_w2, gla_b2,
              gla_norm, s5_lam_re, s5_lam_im, s5_log_step, s5_b_re, s5_b_im, s5_c_re, s5_c_im,
              s5_d, s5_w_glu, s5_b_glu, w_gate, b_gate, w_branch, w_o, ffn_norm, w_up,
              ffn_conv_w, ffn_conv_b, w_down, ple_norm, w_ple_gate, w_ple_proj, final_norm):
    for i in range(DEPTH):
        h = rmsnorm(x, attn_norm[i])
        x = x + token_mixer(h, w_in[i], dn_conv_w[i], dn_a_log[i], dn_dt_bias[i], dn_norm[i],
                            gla_w2[i], gla_b2[i], gla_norm[i], s5_lam_re[i], s5_lam_im[i],
                            s5_log_step[i], s5_b_re[i], s5_b_im[i], s5_c_re[i], s5_c_im[i],
                            s5_d[i], s5_w_glu[i], s5_b_glu[i], w_gate[i], b_gate[i],
                            w_branch[i], w_o[i])
        h = rmsnorm(x, ffn_norm[i])
        x = x + conv_glu(h, w_up[i], ffn_conv_w[i], ffn_conv_b[i], w_down[i])
        ple_gate = jax.nn.sigmoid(rmsnorm(x, ple_norm[i]) @ w_ple_gate[i])
        x = x + ple_gate * (p[i] @ w_ple_proj[i])
    return rmsnorm(x, final_norm)
```

```python
import functools
import math

import jax
import jax.numpy as jnp
from jax import lax
from jax.experimental import pallas as pl
from jax.experimental.pallas import tpu as pltpu

F32 = jnp.float32
BF16 = jnp.bfloat16

EPS = 1e-6
CHUNK = 64
DN_HEADS, DN_DK, DN_DV, DN_CONV = 4, 128, 128, 4
GLA_HEADS, GLA_DK, GLA_DV, GLA_RANK = 4, 64, 128, 16
GLA_NORMALIZER = 16.0
S5_WIDTH, S5_GROUP, S5_STATE = 512, 16, 64
S5_GROUPS = S5_WIDTH // S5_GROUP
FFN_CONV = 3
LANES = 128
HALO = 8

Z_QKV, Z_GATE, Z_GQK, Z_GV, Z_GR, Z_S5U, Z_SMALL = 0, 1536, 2048, 2560, 3072, 3584, 4096
Z_WIDTH = 4224
SM_B, SM_A, SM_LR = 0, 4, 8

VMEM_LIMIT = 56 * 1024 * 1024


def _cparams(sem):
    return pltpu.CompilerParams(dimension_semantics=sem, vmem_limit_bytes=VMEM_LIMIT)


def _mm(a, b):
    return jnp.dot(a.astype(BF16), b.astype(BF16), preferred_element_type=F32)


def _mm_nt(a, b):
    nb = a.ndim - 2
    batch = tuple(range(nb))
    return lax.dot_general(a.astype(BF16), b.astype(BF16),
                           (((a.ndim - 1,), (b.ndim - 1,)), (batch, batch)),
                           preferred_element_type=F32)


def _mm_nn(a, b):
    nb = a.ndim - 2
    batch = tuple(range(nb))
    return lax.dot_general(a.astype(BF16), b.astype(BF16),
                           (((a.ndim - 1,), (b.ndim - 2,)), (batch, batch)),
                           preferred_element_type=F32)


def _mm_tn(a, b):
    nb = a.ndim - 2
    batch = tuple(range(nb))
    return lax.dot_general(a.astype(BF16), b.astype(BF16),
                           (((a.ndim - 2,), (b.ndim - 2,)), (batch, batch)),
                           preferred_element_type=F32)


def _split3(a):
    a1 = a.astype(BF16)
    r1 = a - a1.astype(F32)
    a2 = r1.astype(BF16)
    a3 = (r1 - a2.astype(F32)).astype(BF16)
    return a1, a2, a3


def _cumsum_rows(lmat, a):
    a1, a2, a3 = _split3(a)
    d = lambda p: jnp.dot(lmat, p, preferred_element_type=F32)
    return d(a1) + d(a2) + d(a3)


def _mm_nn3(a, b):
    a1 = a.astype(BF16)
    a2 = (a - a1.astype(F32)).astype(BF16)
    b1 = b.astype(BF16)
    b2 = (b - b1.astype(F32)).astype(BF16)
    return _mm_nn(a1, b1) + _mm_nn(a1, b2) + _mm_nn(a2, b1)


def _rms(xf, g):
    ms = jnp.mean(xf * xf, axis=-1, keepdims=True)
    return xf * lax.rsqrt(ms + EPS) * g


def _sigmoid(x):
    return 1.0 / (1.0 + jnp.exp(-x))


def _softplus(x):
    return jnp.maximum(x, 0.0) + jnp.log(1.0 + jnp.exp(-jnp.abs(x)))


def _gelu_tanh(x):
    c = math.sqrt(2.0 / math.pi)
    return 0.5 * x * (1.0 + jnp.tanh(c * (x + 0.044715 * (x * x * x))))


def _chunk_tril(tb):
    r = lax.broadcasted_iota(jnp.int32, (tb, tb), 0)
    c = lax.broadcasted_iota(jnp.int32, (tb, tb), 1)
    same = (r // CHUNK) == (c // CHUNK)
    return jnp.where(same & (c <= r), 1.0, 0.0).astype(BF16)


def _inproj_kernel(x_ref, g_ref, w_ref, o_ref, h_ref):
    @pl.when(pl.program_id(1) == 0)
    def _():
        h_ref[...] = _rms(x_ref[...], g_ref[...]).astype(BF16)

    o_ref[...] = jnp.dot(h_ref[...], w_ref[...], preferred_element_type=F32)


def _inproj(x2, g, w, tm, tn):
    t, d = x2.shape
    n = w.shape[1]
    return pl.pallas_call(
        _inproj_kernel,
        grid=(t // tm, n // tn),
        in_specs=[pl.BlockSpec((tm, d), lambda i, j: (i, 0)),
                  pl.BlockSpec((1, d), lambda i, j: (0, 0)),
                  pl.BlockSpec((d, tn), lambda i, j: (0, j))],
        out_specs=pl.BlockSpec((tm, tn), lambda i, j: (i, j)),
        out_shape=jax.ShapeDtypeStruct((t, n), F32),
        scratch_shapes=[pltpu.VMEM((tm, d), BF16)],
        compiler_params=_cparams(("parallel", "arbitrary")),
        name="inproj",
    )(x2, g, w)


def _deltanet_kernel(qkv_ref, sm_ref, gate_ref, cw_ref, alog_ref, dtb_ref, ng_ref, o_ref,
                     ext_ref, s_ref, *, tb):
    nh, dk, dv = DN_HEADS, DN_DK, DN_DV
    qk_w = nh * dk

    @pl.when(pl.program_id(1) == 0)
    def _():
        ext_ref[0:HALO, :] = jnp.zeros((HALO, ext_ref.shape[1]), F32)
        s_ref[...] = jnp.zeros(s_ref.shape, F32)

    ext_ref[HALO:HALO + tb, :] = qkv_ref[0]
    acc = cw_ref[0:1, :] * ext_ref[pl.ds(HALO - 3, tb), :]
    for kk in range(1, DN_CONV):
        acc = acc + cw_ref[kk:kk + 1, :] * ext_ref[pl.ds(HALO - 3 + kk, tb), :]
    ext_ref[0:HALO, :] = ext_ref[tb:tb + HALO, :]
    qkv = acc * _sigmoid(acc)

    sm = sm_ref[0]
    lane = lax.broadcasted_iota(jnp.int32, (1, LANES), 1)
    coef = jnp.where((lane >= SM_A) & (lane < SM_A + nh), -jnp.exp(alog_ref[...]), 0.0)
    beta_all = _sigmoid(sm)
    glog = coef * _softplus(sm + dtb_ref[...])
    ltri = _chunk_tril(tb)
    gam_col = _cumsum_rows(ltri, glog)
    gam_row = gam_col.T

    ri = lax.broadcasted_iota(jnp.int32, (CHUNK, CHUNK), 0)
    ci = lax.broadcasted_iota(jnp.int32, (CHUNK, CHUNK), 1)
    incl = (ci <= ri)[None]
    strict = (ci < ri)[None]
    eye = jnp.where(ci == ri, 1.0, 0.0)[None]

    for c in range(tb // CHUNK):
        r0 = c * CHUNK
        rows = slice(r0, r0 + CHUNK)
        q = jnp.stack([qkv[rows, h * dk:(h + 1) * dk] for h in range(nh)])
        k = jnp.stack([qkv[rows, qk_w + h * dk:qk_w + (h + 1) * dk] for h in range(nh)])
        v = jnp.stack([qkv[rows, 2 * qk_w + h * dv:2 * qk_w + (h + 1) * dv] for h in range(nh)])
        q = q * lax.rsqrt(jnp.sum(q * q, axis=-1, keepdims=True) + EPS) * (dk ** -0.5)
        k = k * lax.rsqrt(jnp.sum(k * k, axis=-1, keepdims=True) + EPS)
        gcol = jnp.stack([gam_col[rows, SM_A + h:SM_A + h + 1] for h in range(nh)])
        grow = jnp.stack([gam_row[SM_A + h:SM_A + h + 1, rows] for h in range(nh)])
        bcol = jnp.stack([beta_all[rows, SM_B + h:SM_B + h + 1] for h in range(nh)])

        dec = jnp.exp(jnp.where(incl, gcol - grow, -jnp.inf))
        kkm = _mm_nt(k, k)
        a_neg = -(bcol * kkm * jnp.where(strict, dec, 0.0))
        tinv = eye + a_neg
        mp = a_neg
        for _ in range(5):
            mp = _mm_nn3(mp, mp)
            tinv = tinv + _mm_nn3(tinv, mp)
        eg = jnp.exp(gcol)
        rhs = jnp.concatenate([bcol * v, (bcol * eg) * k], axis=-1)
        sol = _mm_nn(tinv, rhs)
        u_new, w = sol[..., :dv], sol[..., dv:]
        attn = _mm_nt(q, k) * dec
        q_dec = q * eg
        gend = gcol[:, CHUNK - 1:CHUNK, :]
        k_dec = k * jnp.exp(gend - gcol)

        s_old = s_ref[...]
        u = u_new - _mm_nn(w, s_old)
        o = _mm_nn(q_dec, s_old) + _mm_nn(attn, u)
        s_ref[...] = jnp.exp(gend) * s_old + _mm_tn(k_dec, u)

        o = _rms(o, ng_ref[...][None])
        for h in range(nh):
            g = gate_ref[0, rows, h * dv:(h + 1) * dv]
            o_ref[0, rows, h * dv:(h + 1) * dv] = (o[h] * (g * _sigmoid(g))).astype(o_ref.dtype)


def _deltanet(z3, conv_w, alog_row, dtb_row, norm_g, tb):
    b, l, _ = z3.shape
    w_qkv = 3 * DN_HEADS * DN_DK
    kern = functools.partial(_deltanet_kernel, tb=tb)
    const = lambda bi, j: (0, 0)
    return pl.pallas_call(
        kern,
        grid=(b, l // tb),
        in_specs=[pl.BlockSpec((1, tb, w_qkv), lambda bi, j: (bi, j, Z_QKV // w_qkv)),
                  pl.BlockSpec((1, tb, LANES), lambda bi, j: (bi, j, Z_SMALL // LANES)),
                  pl.BlockSpec((1, tb, 512), lambda bi, j: (bi, j, Z_GATE // 512)),
                  pl.BlockSpec((DN_CONV, w_qkv), const),
                  pl.BlockSpec((1, LANES), const),
                  pl.BlockSpec((1, LANES), const),
                  pl.BlockSpec((1, DN_DV), const)],
        out_specs=pl.BlockSpec((1, tb, DN_HEADS * DN_DV), lambda bi, j: (bi, j, 0)),
        out_shape=jax.ShapeDtypeStruct((b, l, DN_HEADS * DN_DV), BF16),
        scratch_shapes=[pltpu.VMEM((tb + HALO, w_qkv), F32),
                        pltpu.VMEM((DN_HEADS, DN_DK, DN_DV), F32)],
        compiler_params=_cparams(("parallel", "arbitrary")),
        name="deltanet",
    )(z3, z3, z3, conv_w, alog_row, dtb_row, norm_g)


def _gla_kernel(qk_ref, v_ref, r_ref, sm_ref, w2_ref, b2_ref, ng_ref, o_ref, st_ref, *, tb):
    nh, dk, dv = GLA_HEADS, GLA_DK, GLA_DV
    qk_w = nh * dk

    @pl.when(pl.program_id(1) == 0)
    def _():
        st_ref[...] = jnp.zeros(st_ref.shape, F32)

    z = _mm(sm_ref[0], w2_ref[...]) + b2_ref[...]
    glog = -_softplus(-z) * (1.0 / GLA_NORMALIZER)
    bcum = _cumsum_rows(_chunk_tril(tb), glog)
    qk = qk_ref[0]
    q = qk[:, :qk_w] * (dk ** -0.5)
    k = qk[:, qk_w:]
    q_e = q * jnp.exp(bcum)
    k_e = k * jnp.exp(-bcum)

    ri = lax.broadcasted_iota(jnp.int32, (CHUNK, CHUNK), 0)
    ci = lax.broadcasted_iota(jnp.int32, (CHUNK, CHUNK), 1)
    incl = ci <= ri

    for c in range(tb // CHUNK):
        r0 = c * CHUNK
        rows = slice(r0, r0 + CHUNK)
        b_end = bcum[r0 + CHUNK - 1:r0 + CHUNK, :]
        k_dec = k[rows] * jnp.exp(b_end - bcum[rows])
        g_end = jnp.exp(b_end)
        for h in range(nh):
            hs = slice(h * dk, (h + 1) * dk)
            vs = slice(h * dv, (h + 1) * dv)
            qe_h = q_e[rows, hs]
            v_h = v_ref[0, rows, vs]
            attn = jnp.where(incl, _mm_nt(qe_h, k_e[rows, hs]), 0.0)
            st = st_ref[h]
            o = _mm(attn, v_h) + _mm_nt(qe_h, st)
            st_ref[h] = g_end[:, hs] * st + _mm_tn(v_h, k_dec[:, hs])
            o = _rms(o, ng_ref[...])
            g = r_ref[0, rows, vs]
            o_ref[0, rows, vs] = (o * (g * _sigmoid(g))).astype(o_ref.dtype)


def _gla(z3, w2p, b2, norm_g, tb):
    b, l, _ = z3.shape
    kern = functools.partial(_gla_kernel, tb=tb)
    const = lambda bi, j: (0, 0)
    return pl.pallas_call(
        kern,
        grid=(b, l // tb),
        in_specs=[pl.BlockSpec((1, tb, 512), lambda bi, j: (bi, j, Z_GQK // 512)),
                  pl.BlockSpec((1, tb, 512), lambda bi, j: (bi, j, Z_GV // 512)),
                  pl.BlockSpec((1, tb, 512), lambda bi, j: (bi, j, Z_GR // 512)),
                  pl.BlockSpec((1, tb, LANES), lambda bi, j: (bi, j, Z_SMALL // LANES)),
                  pl.BlockSpec((LANES, GLA_HEADS * GLA_DK), const),
                  pl.BlockSpec((1, GLA_HEADS * GLA_DK), const),
                  pl.BlockSpec((1, GLA_DV), const)],
        out_specs=pl.BlockSpec((1, tb, GLA_HEADS * GLA_DV), lambda bi, j: (bi, j, 0)),
        out_shape=jax.ShapeDtypeStruct((b, l, GLA_HEADS * GLA_DV), BF16),
        scratch_shapes=[pltpu.VMEM((GLA_HEADS, GLA_DV, GLA_DK), F32)],
        compiler_params=_cparams(("parallel", "arbitrary")),
        name="gla",
    )(z3, z3, z3, z3, w2p, b2, norm_g)


def _s5_kernel(u_ref, m_ref, e_ref, f_ref, ap_ref, y_ref, es_ref, hp_ref, *, nbatch):
    u = u_ref[0]
    nrow = u.shape[0]
    nchunk = nrow // nbatch
    es_ref[...] = jnp.dot(u, e_ref[0], preferred_element_type=F32)
    ar = ap_ref[0, 0:1, :]
    ai = ap_ref[0, 1:2, :]

    def body(g, carry):
        new = []
        for b in range(nbatch):
            hr, hi = carry[2 * b], carry[2 * b + 1]
            base = pl.multiple_of(b * nchunk + g * HALO, HALO)
            et = es_ref[pl.ds(base, HALO), :]
            hrs, his = [], []
            for i in range(HALO):
                hrs.append(hr)
                his.append(hi)
                er, ei = et[i:i + 1, 0:LANES], et[i:i + 1, LANES:2 * LANES]
                hr, hi = ar * hr - ai * hi + er, ar * hi + ai * hr + ei
            hp_ref[pl.ds(base, HALO), 0:LANES] = jnp.concatenate(hrs, axis=0)
            hp_ref[pl.ds(base, HALO), LANES:2 * LANES] = jnp.concatenate(his, axis=0)
            new += [hr, hi]
        return tuple(new)

    zero = jnp.zeros((1, LANES), F32)
    lax.fori_loop(0, nchunk // HALO, body, (zero,) * (2 * nbatch))
    y = jnp.dot(u, m_ref[0], preferred_element_type=F32)
    y_ref[0] = y + jnp.dot(hp_ref[...].astype(BF16), f_ref[0], preferred_element_type=F32)


def _s5(ut, m, e, f, ap, nbatch):
    g, r, w = ut.shape
    kern = functools.partial(_s5_kernel, nbatch=nbatch)
    blk = lambda *shape: pl.BlockSpec((1,) + shape, lambda i: (i, 0, 0))
    return pl.pallas_call(
        kern,
        grid=(g,),
        in_specs=[blk(r, w), blk(w, w), blk(w, 2 * LANES), blk(2 * LANES, w), blk(HALO, LANES)],
        out_specs=blk(r, w),
        out_shape=jax.ShapeDtypeStruct((g, r, w), F32),
        scratch_shapes=[pltpu.VMEM((r, 2 * LANES), F32), pltpu.VMEM((r, 2 * LANES), F32)],
        compiler_params=_cparams(("parallel",)),
        name="s5",
    )(ut, m, e, f, ap)


def _s5_operators(lam_re, lam_im, log_step, b_re, b_im, c_re, c_im):
    hp = lax.Precision.HIGHEST
    tc, n = CHUNK, S5_STATE
    step = jnp.exp(log_step)[:, None]
    lr, li = lam_re, lam_im
    mag = jnp.exp(lr * step)
    abar_re, abar_im = mag * jnp.cos(li * step), mag * jnp.sin(li * step)
    den = lr * lr + li * li
    nr, ni = abar_re - 1.0, abar_im
    fr = (nr * lr + ni * li) / den
    fi = (ni * lr - nr * li) / den
    bb_re = fr[..., None] * b_re - fi[..., None] * b_im
    bb_im = fr[..., None] * b_im + fi[..., None] * b_re
    tau = jnp.arange(tc + 1, dtype=F32)[:, None, None]
    pmag = jnp.exp(lr * step * tau)
    p_re, p_im = pmag * jnp.cos(li * step * tau), pmag * jnp.sin(li * step * tau)
    ca_re = c_re[None] * p_re[:tc, :, None, :] - c_im[None] * p_im[:tc, :, None, :]
    ca_im = c_re[None] * p_im[:tc, :, None, :] + c_im[None] * p_re[:tc, :, None, :]
    kern = (jnp.einsum('tgin,gnj->tgij', ca_re, bb_re, precision=hp)
            - jnp.einsum('tgin,gnj->tgij', ca_im, bb_im, precision=hp))
    t_idx = jnp.arange(tc)
    lag = t_idx[None, :] - t_idx[:, None]
    kt = kern[jnp.clip(lag, 0, tc - 1)]
    kt = jnp.where((lag >= 0)[:, :, None, None, None], kt, 0.0)
    m = kt.transpose(2, 0, 4, 1, 3).reshape(S5_GROUPS, tc * S5_GROUP, tc * S5_GROUP)
    q_re, q_im = p_re[:tc][::-1], p_im[:tc][::-1]
    e_re = q_re[..., None] * bb_re[None] - q_im[..., None] * bb_im[None]
    e_im = q_re[..., None] * bb_im[None] + q_im[..., None] * bb_re[None]
    pad = jnp.zeros((S5_GROUPS, tc * S5_GROUP, LANES - n), F32)
    to_e = lambda x: x.transpose(1, 0, 3, 2).reshape(S5_GROUPS, tc * S5_GROUP, n)
    e = jnp.concatenate([to_e(e_re), pad, to_e(e_im), pad], axis=-1)
    r_re, r_im = p_re[1:], p_im[1:]
    f_re = c_re[None] * r_re[:, :, None, :] - c_im[None] * r_im[:, :, None, :]
    f_im = c_re[None] * r_im[:, :, None, :] + c_im[None] * r_re[:, :, None, :]
    to_f = lambda x: x.transpose(1, 3, 0, 2).reshape(S5_GROUPS, n, tc * S5_GROUP)
    padf = jnp.zeros((S5_GROUPS, LANES - n, tc * S5_GROUP), F32)
    f = jnp.concatenate([to_f(f_re), padf, -to_f(f_im), padf], axis=1)
    ap = jnp.zeros((S5_GROUPS, HALO, LANES), F32)
    ap = ap.at[:, 0, :n].set(p_re[tc]).at[:, 1, :n].set(p_im[tc])
    return m.astype(BF16), e.astype(BF16), f.astype(BF16), ap


def _merge_kernel(x_ref, ya_ref, yb_ref, ys_ref, u_ref, ng_ref, wg_ref, bg_ref, wb_ref, wo_ref,
                  d_ref, wglu_ref, bglu_ref, o_ref):
    x = x_ref[...]
    d = x.shape[1]
    h = _rms(x, ng_ref[...]).astype(BF16)
    yc = _gelu_tanh(ys_ref[...] + d_ref[...] * u_ref[...])
    yc = yc * _sigmoid(_mm(yc, wglu_ref[...]) + bglu_ref[...])
    merged = None
    for i, y in enumerate((ya_ref[...], yb_ref[...], yc.astype(BF16))):
        gate = _sigmoid(jnp.dot(h, wg_ref[:, i * d:(i + 1) * d], preferred_element_type=F32)
                        + bg_ref[:, i * d:(i + 1) * d])
        term = gate * jnp.dot(y, wb_ref[i], preferred_element_type=F32)
        merged = term if merged is None else merged + term
    o_ref[...] = x + _mm(merged, wo_ref[...])


def _merge(x2, ya, yb, ys, z2, ng, wg, bg, wb, wo, d, wglu, bglu, tm):
    t, dm = x2.shape
    bw = ya.shape[1]
    row = lambda w: pl.BlockSpec((tm, w), lambda i: (i, 0))
    const2 = lambda a: pl.BlockSpec(a.shape, lambda i: (0,) * a.ndim)
    return pl.pallas_call(
        _merge_kernel,
        grid=(t // tm,),
        in_specs=[row(dm), row(bw), row(bw), row(bw),
                  pl.BlockSpec((tm, S5_WIDTH), lambda i: (i, Z_S5U // S5_WIDTH)),
                  const2(ng), const2(wg), const2(bg), const2(wb), const2(wo),
                  const2(d), const2(wglu), const2(bglu)],
        out_specs=row(dm),
        out_shape=jax.ShapeDtypeStruct((t, dm), F32),
        compiler_params=_cparams(("parallel",)),
        name="merge",
    )(x2, ya, yb, ys, z2, ng, wg, bg, wb, wo, d, wglu, bglu)


def _ffn_kernel(x_ref, p_ref, ng_ref, wug_ref, wuu_ref, cw_ref, cb_ref, wd_ref, png_ref, wpg_ref,
                wpp_ref, fg_ref, o_ref, h_ref, acc_ref, ext_ref, *, tm, tiles_per_seq, final):
    i = pl.program_id(0)
    j = pl.program_id(1)
    nj = pl.num_programs(1)

    @pl.when(j == 0)
    def _():
        h_ref[...] = _rms(x_ref[...], ng_ref[...]).astype(BF16)
        acc_ref[...] = jnp.zeros(acc_ref.shape, F32)

    @pl.when(i % tiles_per_seq == 0)
    def _():
        ext_ref[j, 0:HALO, :] = jnp.zeros((HALO, ext_ref.shape[2]), F32)

    h = h_ref[...]
    ext_ref[j, HALO:HALO + tm, :] = jnp.dot(h, wug_ref[...], preferred_element_type=F32)
    u = jnp.dot(h, wuu_ref[...], preferred_element_type=F32)
    g = cb_ref[...] + cw_ref[0:1, :] * ext_ref[j, pl.ds(HALO - 2, tm), :]
    for kk in range(1, FFN_CONV):
        g = g + cw_ref[kk:kk + 1, :] * ext_ref[j, pl.ds(HALO - 2 + kk, tm), :]
    ext_ref[j, 0:HALO, :] = ext_ref[j, tm:tm + HALO, :]
    acc_ref[...] += _mm(_gelu_tanh(g) * u, wd_ref[...])

    @pl.when(j == nj - 1)
    def _():
        xn = x_ref[...] + acc_ref[...]
        gate = _sigmoid(_mm(_rms(xn, png_ref[...]), wpg_ref[...]))
        out = xn + gate * _mm(p_ref[...], wpp_ref[...])
        if final:
            out = _rms(out, fg_ref[...])
        o_ref[...] = out


def _ffn(x2, p2, ng, wug, wuu, cw, cb, wd, png, wpg, wpp, fg, seq, tm, nff, final):
    t, dm = x2.shape
    dff = wd.shape[0]
    ffc = dff // nff
    kern = functools.partial(_ffn_kernel, tm=tm, tiles_per_seq=seq // tm, final=final)
    const = lambda a: pl.BlockSpec(a.shape, lambda i, j: (0,) * a.ndim)
    return pl.pallas_call(
        kern,
        grid=(t // tm, nff),
        in_specs=[pl.BlockSpec((tm, dm), lambda i, j: (i, 0)),
                  pl.BlockSpec((tm, p2.shape[1]), lambda i, j: (i, 0)),
                  const(ng),
                  pl.BlockSpec((dm, ffc), lambda i, j: (0, j)),
                  pl.BlockSpec((dm, ffc), lambda i, j: (0, j)),
                  pl.BlockSpec((FFN_CONV, ffc), lambda i, j: (0, j)),
                  pl.BlockSpec((1, ffc), lambda i, j: (0, j)),
                  pl.BlockSpec((ffc, dm), lambda i, j: (j, 0)),
                  const(png), const(wpg), const(wpp), const(fg)],
        out_specs=pl.BlockSpec((tm, dm), lambda i, j: (i, 0)),
        out_shape=jax.ShapeDtypeStruct((t, dm), F32),
        scratch_shapes=[pltpu.VMEM((tm, dm), BF16),
                        pltpu.VMEM((tm, dm), F32),
                        pltpu.VMEM((nff, tm + HALO, ffc), F32)],
        compiler_params=_cparams(("arbitrary", "arbitrary")),
        name="ffn",
    )(x2, p2, ng, wug, wuu, cw, cb, wd, png, wpg, wpp, fg)


def _reorder_w_in(w_in):
    qk, v = DN_HEADS * DN_DK, DN_HEADS * DN_DV
    gqk, gv = GLA_HEADS * GLA_DK, GLA_HEADS * GLA_DV
    o = 0
    seg = {}
    for name, width in (("dn_q", qk), ("dn_k", qk), ("dn_v", v), ("dn_b", DN_HEADS), ("dn_a", DN_HEADS),
                        ("dn_g", v), ("gl_q", gqk), ("gl_k", gqk), ("gl_v", gv), ("gl_lr", GLA_RANK),
                        ("gl_r", gv), ("s5_u", S5_WIDTH)):
        seg[name] = w_in[:, o:o + width]
        o += width
    d = w_in.shape[0]
    small = jnp.concatenate([seg["dn_b"], seg["dn_a"], seg["gl_lr"],
                             jnp.zeros((d, LANES - 2 * DN_HEADS - GLA_RANK), w_in.dtype)], axis=1)
    return jnp.concatenate([seg["dn_q"], seg["dn_k"], seg["dn_v"], seg["dn_g"], seg["gl_q"], seg["gl_k"],
                            seg["gl_v"], seg["gl_r"], seg["s5_u"], small], axis=1)


def _lane_row(vals, offset):
    return jnp.zeros((1, LANES), F32).at[0, offset:offset + vals.shape[0]].set(vals)


def kernel(x, p, attn_norm, w_in, dn_conv_w, dn_a_log, dn_dt_bias, dn_norm, gla_w2, gla_b2, gla_norm, s5_lam_re, s5_lam_im, s5_log_step, s5_b_re, s5_b_im, s5_c_re, s5_c_im, s5_d, s5_w_glu, s5_b_glu, w_gate, b_gate, w_branch, w_o, ffn_norm, w_up, ffn_conv_w, ffn_conv_b, w_down, ple_norm, w_ple_gate, w_ple_proj, final_norm):
    bsz, seq, dm = x.shape
    depth = w_in.shape[0]
    t = bsz * seq
    dff = w_down.shape[1]
    nchunk = seq // CHUNK
    tb = min(256, seq)
    tm = min(512, seq)
    x2 = x.reshape(t, dm)
    row = lambda a: a.reshape(1, -1)
    for i in range(depth):
        w_in_r = _reorder_w_in(w_in[i]).astype(BF16)
        z2 = _inproj(x2, row(attn_norm[i]), w_in_r, tm=min(1024, t), tn=Z_WIDTH // 3)
        z3 = z2.reshape(bsz, seq, Z_WIDTH)
        ya = _deltanet(z3, dn_conv_w[i], _lane_row(dn_a_log[i], SM_A), _lane_row(dn_dt_bias[i], SM_A),
                       row(dn_norm[i]), tb)
        w2p = jnp.zeros((LANES, GLA_HEADS * GLA_DK), F32).at[SM_LR:SM_LR + GLA_RANK].set(gla_w2[i])
        yb = _gla(z3, w2p.astype(BF16), row(gla_b2[i]), row(gla_norm[i]), tb)
        m_op, e_op, f_op, ap = _s5_operators(s5_lam_re[i], s5_lam_im[i], s5_log_step[i], s5_b_re[i],
                                             s5_b_im[i], s5_c_re[i], s5_c_im[i])
        ut = z3[:, :, Z_S5U:Z_S5U + S5_WIDTH].reshape(bsz, nchunk, CHUNK, S5_GROUPS, S5_GROUP)
        ut = ut.transpose(3, 0, 1, 2, 4).reshape(S5_GROUPS, bsz * nchunk, CHUNK * S5_GROUP).astype(BF16)
        yt = _s5(ut, m_op, e_op, f_op, ap, bsz)
        ys = yt.reshape(S5_GROUPS, bsz, nchunk, CHUNK, S5_GROUP).transpose(1, 2, 3, 0, 4).reshape(t, S5_WIDTH)
        x2 = _merge(x2, ya.reshape(t, -1), yb.reshape(t, -1), ys, z2, row(attn_norm[i]),
                    w_gate[i].astype(BF16), row(b_gate[i]), w_branch[i].astype(BF16), w_o[i].astype(BF16),
                    row(s5_d[i]), s5_w_glu[i].astype(BF16), row(s5_b_glu[i]), tm)
        x2 = _ffn(x2, p[i].reshape(t, -1), row(ffn_norm[i]), w_up[i, :, :dff].astype(BF16),
                  w_up[i, :, dff:].astype(BF16), ffn_conv_w[i], row(ffn_conv_b[i]), w_down[i].astype(BF16),
                  row(ple_norm[i]), w_ple_gate[i].astype(BF16), w_ple_proj[i].astype(BF16),
                  row(final_norm), seq, tm, 2, i == depth - 1)
    return x2.reshape(bsz, seq, dm)
```

```python
import functools
import math

import jax
import jax.numpy as jnp
from jax import lax
from jax.experimental import pallas as pl
from jax.experimental.pallas import tpu as pltpu

F32 = jnp.float32
BF16 = jnp.bfloat16

EPS = 1e-6
CHUNK = 64
DN_HEADS, DN_DK, DN_DV, DN_CONV = 4, 128, 128, 4
GLA_HEADS, GLA_DK, GLA_DV, GLA_RANK = 4, 64, 128, 16
GLA_NORMALIZER = 16.0
S5_WIDTH, S5_GROUP, S5_STATE = 512, 16, 64
S5_GROUPS = S5_WIDTH // S5_GROUP
S5_TC = 128
FFN_CONV = 3
LANES = 128
HALO = 8

Z_QKV, Z_GATE, Z_GQK, Z_GV, Z_GR, Z_S5U, Z_SMALL = 0, 1536, 2048, 2560, 3072, 3584, 4096
Z_WIDTH = 4224
SM_B, SM_A, SM_LR = 0, 4, 8

VMEM_LIMIT = 56 * 1024 * 1024


def _cparams(sem):
    return pltpu.CompilerParams(dimension_semantics=sem, vmem_limit_bytes=VMEM_LIMIT)


def _mm(a, b):
    return jnp.dot(a.astype(BF16), b.astype(BF16), preferred_element_type=F32)


def _mm_nt(a, b):
    nb = a.ndim - 2
    batch = tuple(range(nb))
    return lax.dot_general(a.astype(BF16), b.astype(BF16),
                           (((a.ndim - 1,), (b.ndim - 1,)), (batch, batch)),
                           preferred_element_type=F32)


def _mm_nn(a, b):
    nb = a.ndim - 2
    batch = tuple(range(nb))
    return lax.dot_general(a.astype(BF16), b.astype(BF16),
                           (((a.ndim - 1,), (b.ndim - 2,)), (batch, batch)),
                           preferred_element_type=F32)


def _mm_tn(a, b):
    nb = a.ndim - 2
    batch = tuple(range(nb))
    return lax.dot_general(a.astype(BF16), b.astype(BF16),
                           (((a.ndim - 2,), (b.ndim - 2,)), (batch, batch)),
                           preferred_element_type=F32)


def _split3(a):
    a1 = a.astype(BF16)
    r1 = a - a1.astype(F32)
    a2 = r1.astype(BF16)
    a3 = (r1 - a2.astype(F32)).astype(BF16)
    return a1, a2, a3


def _cumsum_rows(lmat, a):
    a1, a2, a3 = _split3(a)
    d = lambda p: jnp.dot(lmat, p, preferred_element_type=F32)
    return d(a1) + d(a2) + d(a3)


def _mm_nn3(a, b):
    a1 = a.astype(BF16)
    a2 = (a - a1.astype(F32)).astype(BF16)
    b1 = b.astype(BF16)
    b2 = (b - b1.astype(F32)).astype(BF16)
    return _mm_nn(a1, b1) + _mm_nn(a1, b2) + _mm_nn(a2, b1)


def _rms(xf, g):
    ms = jnp.mean(xf * xf, axis=-1, keepdims=True)
    return xf * lax.rsqrt(ms + EPS) * g


def _sigmoid(x):
    return 1.0 / (1.0 + jnp.exp(-x))


def _softplus(x):
    return jnp.maximum(x, 0.0) + jnp.log(1.0 + jnp.exp(-jnp.abs(x)))


def _gelu_tanh(x):
    c = math.sqrt(2.0 / math.pi)
    return 0.5 * x * (1.0 + jnp.tanh(c * (x + 0.044715 * (x * x * x))))


def _chunk_tril(tb):
    r = lax.broadcasted_iota(jnp.int32, (tb, tb), 0)
    c = lax.broadcasted_iota(jnp.int32, (tb, tb), 1)
    same = (r // CHUNK) == (c // CHUNK)
    return jnp.where(same & (c <= r), 1.0, 0.0).astype(BF16)


def _inproj_kernel(x_ref, g_ref, w_ref, o_ref, ut_ref, h_ref, *, s5_off):
    j = pl.program_id(1)

    @pl.when(j == 0)
    def _():
        h_ref[...] = _rms(x_ref[...], g_ref[...]).astype(BF16)

    o = jnp.dot(h_ref[...], w_ref[...], preferred_element_type=F32)
    o_ref[...] = o

    @pl.when(j == pl.num_programs(1) - 1)
    def _():
        ut_ref[...] = o[:, s5_off:s5_off + S5_WIDTH].T.astype(BF16)


def _inproj(x2, g, w, tm, tn):
    t, d = x2.shape
    n = w.shape[1]
    nj = n // tn
    s5_off = Z_S5U - (nj - 1) * tn
    assert 0 <= s5_off and s5_off + S5_WIDTH <= tn
    return pl.pallas_call(
        functools.partial(_inproj_kernel, s5_off=s5_off),
        grid=(t // tm, nj),
        in_specs=[pl.BlockSpec((tm, d), lambda i, j: (i, 0)),
                  pl.BlockSpec((1, d), lambda i, j: (0, 0)),
                  pl.BlockSpec((d, tn), lambda i, j: (0, j))],
        out_specs=[pl.BlockSpec((tm, tn), lambda i, j: (i, j)),
                   pl.BlockSpec((S5_WIDTH, tm), lambda i, j: (0, i))],
        out_shape=[jax.ShapeDtypeStruct((t, n), F32),
                   jax.ShapeDtypeStruct((S5_WIDTH, t), BF16)],
        scratch_shapes=[pltpu.VMEM((tm, d), BF16)],
        compiler_params=_cparams(("parallel", "arbitrary")),
        name="inproj",
    )(x2, g, w)


def _deltanet_kernel(qkv_ref, sm_ref, gate_ref, cw_ref, alog_ref, dtb_ref, ng_ref, o_ref,
                     ext_ref, s_ref, *, tb):
    nh, dk, dv = DN_HEADS, DN_DK, DN_DV
    qk_w = nh * dk

    @pl.when(pl.program_id(1) == 0)
    def _():
        ext_ref[0:HALO, :] = jnp.zeros((HALO, ext_ref.shape[1]), F32)
        s_ref[...] = jnp.zeros(s_ref.shape, F32)

    ext_ref[HALO:HALO + tb, :] = qkv_ref[0]
    acc = cw_ref[0:1, :] * ext_ref[pl.ds(HALO - 3, tb), :]
    for kk in range(1, DN_CONV):
        acc = acc + cw_ref[kk:kk + 1, :] * ext_ref[pl.ds(HALO - 3 + kk, tb), :]
    ext_ref[0:HALO, :] = ext_ref[tb:tb + HALO, :]
    qkv = acc * _sigmoid(acc)

    sm = sm_ref[0]
    lane = lax.broadcasted_iota(jnp.int32, (1, LANES), 1)
    coef = jnp.where((lane >= SM_A) & (lane < SM_A + nh), -jnp.exp(alog_ref[...]), 0.0)
    beta_all = _sigmoid(sm)
    glog = coef * _softplus(sm + dtb_ref[...])
    ltri = _chunk_tril(tb)
    gam_col = _cumsum_rows(ltri, glog)
    gam_row = gam_col.T

    ri = lax.broadcasted_iota(jnp.int32, (CHUNK, CHUNK), 0)
    ci = lax.broadcasted_iota(jnp.int32, (CHUNK, CHUNK), 1)
    incl = (ci <= ri)[None]
    strict = (ci < ri)[None]
    eye = jnp.where(ci == ri, 1.0, 0.0)[None]

    for c in range(tb // CHUNK):
        r0 = c * CHUNK
        rows = slice(r0, r0 + CHUNK)
        q = jnp.stack([qkv[rows, h * dk:(h + 1) * dk] for h in range(nh)])
        k = jnp.stack([qkv[rows, qk_w + h * dk:qk_w + (h + 1) * dk] for h in range(nh)])
        v = jnp.stack([qkv[rows, 2 * qk_w + h * dv:2 * qk_w + (h + 1) * dv] for h in range(nh)])
        q = q * lax.rsqrt(jnp.sum(q * q, axis=-1, keepdims=True) + EPS) * (dk ** -0.5)
        k = k * lax.rsqrt(jnp.sum(k * k, axis=-1, keepdims=True) + EPS)
        gcol = jnp.stack([gam_col[rows, SM_A + h:SM_A + h + 1] for h in range(nh)])
        grow = jnp.stack([gam_row[SM_A + h:SM_A + h + 1, rows] for h in range(nh)])
        bcol = jnp.stack([beta_all[rows, SM_B + h:SM_B + h + 1] for h in range(nh)])

        dec = jnp.exp(jnp.where(incl, gcol - grow, -jnp.inf))
        kkm = _mm_nt(k, k)
        a_neg = -(bcol * kkm * jnp.where(strict, dec, 0.0))
        tinv = eye + a_neg
        mp = a_neg
        for _ in range(5):
            mp = _mm_nn3(mp, mp)
            tinv = tinv + _mm_nn3(tinv, mp)
        eg = jnp.exp(gcol)
        rhs = jnp.concatenate([bcol * v, (bcol * eg) * k], axis=-1)
        sol = _mm_nn(tinv, rhs)
        u_new, w = sol[..., :dv], sol[..., dv:]
        attn = _mm_nt(q, k) * dec
        q_dec = q * eg
        gend = gcol[:, CHUNK - 1:CHUNK, :]
        k_dec = k * jnp.exp(gend - gcol)

        s_old = s_ref[...]
        u = u_new - _mm_nn(w, s_old)
        o = _mm_nn(q_dec, s_old) + _mm_nn(attn, u)
        s_ref[...] = jnp.exp(gend) * s_old + _mm_tn(k_dec, u)

        o = _rms(o, ng_ref[...][None])
        for h in range(nh):
            g = gate_ref[0, rows, h * dv:(h + 1) * dv]
            o_ref[0, rows, h * dv:(h + 1) * dv] = (o[h] * (g * _sigmoid(g))).astype(o_ref.dtype)


def _deltanet(z3, conv_w, alog_row, dtb_row, norm_g, tb):
    b, l, _ = z3.shape
    w_qkv = 3 * DN_HEADS * DN_DK
    kern = functools.partial(_deltanet_kernel, tb=tb)
    const = lambda bi, j: (0, 0)
    return pl.pallas_call(
        kern,
        grid=(b, l // tb),
        in_specs=[pl.BlockSpec((1, tb, w_qkv), lambda bi, j: (bi, j, Z_QKV // w_qkv)),
                  pl.BlockSpec((1, tb, LANES), lambda bi, j: (bi, j, Z_SMALL // LANES)),
                  pl.BlockSpec((1, tb, 512), lambda bi, j: (bi, j, Z_GATE // 512)),
                  pl.BlockSpec((DN_CONV, w_qkv), const),
                  pl.BlockSpec((1, LANES), const),
                  pl.BlockSpec((1, LANES), const),
                  pl.BlockSpec((1, DN_DV), const)],
        out_specs=pl.BlockSpec((1, tb, DN_HEADS * DN_DV), lambda bi, j: (bi, j, 0)),
        out_shape=jax.ShapeDtypeStruct((b, l, DN_HEADS * DN_DV), BF16),
        scratch_shapes=[pltpu.VMEM((tb + HALO, w_qkv), F32),
                        pltpu.VMEM((DN_HEADS, DN_DK, DN_DV), F32)],
        compiler_params=_cparams(("parallel", "arbitrary")),
        name="deltanet",
    )(z3, z3, z3, conv_w, alog_row, dtb_row, norm_g)


def _gla_kernel(qk_ref, v_ref, r_ref, sm_ref, w2_ref, b2_ref, ng_ref, o_ref, st_ref, *, tb):
    nh, dk, dv = GLA_HEADS, GLA_DK, GLA_DV
    qk_w = nh * dk

    @pl.when(pl.program_id(1) == 0)
    def _():
        st_ref[...] = jnp.zeros(st_ref.shape, F32)

    z = _mm(sm_ref[0], w2_ref[...]) + b2_ref[...]
    glog = -_softplus(-z) * (1.0 / GLA_NORMALIZER)
    bcum = _cumsum_rows(_chunk_tril(tb), glog)
    qk = qk_ref[0]
    q = qk[:, :qk_w] * (dk ** -0.5)
    k = qk[:, qk_w:]
    q_e = q * jnp.exp(bcum)
    k_e = k * jnp.exp(-bcum)

    ri = lax.broadcasted_iota(jnp.int32, (CHUNK, CHUNK), 0)
    ci = lax.broadcasted_iota(jnp.int32, (CHUNK, CHUNK), 1)
    incl = ci <= ri

    for c in range(tb // CHUNK):
        r0 = c * CHUNK
        rows = slice(r0, r0 + CHUNK)
        b_end = bcum[r0 + CHUNK - 1:r0 + CHUNK, :]
        k_dec = k[rows] * jnp.exp(b_end - bcum[rows])
        g_end = jnp.exp(b_end)
        for h in range(nh):
            hs = slice(h * dk, (h + 1) * dk)
            vs = slice(h * dv, (h + 1) * dv)
            qe_h = q_e[rows, hs]
            v_h = v_ref[0, rows, vs]
            attn = jnp.where(incl, _mm_nt(qe_h, k_e[rows, hs]), 0.0)
            st = st_ref[h]
            o = _mm(attn, v_h) + _mm_nt(qe_h, st)
            st_ref[h] = g_end[:, hs] * st + _mm_tn(v_h, k_dec[:, hs])
            o = _rms(o, ng_ref[...])
            g = r_ref[0, rows, vs]
            o_ref[0, rows, vs] = (o * (g * _sigmoid(g))).astype(o_ref.dtype)


def _gla(z3, w2p, b2, norm_g, tb):
    b, l, _ = z3.shape
    kern = functools.partial(_gla_kernel, tb=tb)
    const = lambda bi, j: (0, 0)
    return pl.pallas_call(
        kern,
        grid=(b, l // tb),
        in_specs=[pl.BlockSpec((1, tb, 512), lambda bi, j: (bi, j, Z_GQK // 512)),
                  pl.BlockSpec((1, tb, 512), lambda bi, j: (bi, j, Z_GV // 512)),
                  pl.BlockSpec((1, tb, 512), lambda bi, j: (bi, j, Z_GR // 512)),
                  pl.BlockSpec((1, tb, LANES), lambda bi, j: (bi, j, Z_SMALL // LANES)),
                  pl.BlockSpec((LANES, GLA_HEADS * GLA_DK), const),
                  pl.BlockSpec((1, GLA_HEADS * GLA_DK), const),
                  pl.BlockSpec((1, GLA_DV), const)],
        out_specs=pl.BlockSpec((1, tb, GLA_HEADS * GLA_DV), lambda bi, j: (bi, j, 0)),
        out_shape=jax.ShapeDtypeStruct((b, l, GLA_HEADS * GLA_DV), BF16),
        scratch_shapes=[pltpu.VMEM((GLA_HEADS, GLA_DV, GLA_DK), F32)],
        compiler_params=_cparams(("parallel", "arbitrary")),
        name="gla",
    )(z3, z3, z3, z3, w2p, b2, norm_g)


def _s5_kernel(u_ref, p_ref, pw_ref, e_ref, f_ref, ap_ref, y_ref, m_ref, taps_ref, es_ref, hp_ref, *,
               nbatch, ng):
    tc = S5_TC
    nrow = u_ref.shape[1]
    nchunk = nrow // nbatch

    def chunk_rows(g):
        return jnp.concatenate([u_ref[g * S5_GROUP + j] for j in range(S5_GROUP)], axis=-1)

    for g in range(ng):
        es_ref[g] = jnp.dot(chunk_rows(g), e_ref[g], preferred_element_type=F32)

    ars = [ap_ref[g, 0:1, :] for g in range(ng)]
    ais = [ap_ref[g, 1:2, :] for g in range(ng)]

    def body(step, carry):
        new = []
        for g in range(ng):
            for b in range(nbatch):
                hr, hi = carry[2 * (g * nbatch + b)], carry[2 * (g * nbatch + b) + 1]
                base = pl.multiple_of(b * nchunk + step * HALO, HALO)
                et = es_ref[g, pl.ds(base, HALO), :]
                hrs, his = [], []
                for i in range(HALO):
                    hrs.append(hr)
                    his.append(hi)
                    er, ei = et[i:i + 1, 0:LANES], et[i:i + 1, LANES:2 * LANES]
                    hr, hi = ars[g] * hr - ais[g] * hi + er, ars[g] * hi + ais[g] * hr + ei
                hp_ref[g, pl.ds(base, HALO), 0:LANES] = jnp.concatenate(hrs, axis=0)
                hp_ref[g, pl.ds(base, HALO), LANES:2 * LANES] = jnp.concatenate(his, axis=0)
                new += [hr, hi]
        return tuple(new)

    zero = jnp.zeros((1, LANES), F32)
    lax.fori_loop(0, nchunk // HALO, body, (zero,) * (2 * nbatch * ng))

    causal = (lax.broadcasted_iota(jnp.int32, (tc, tc), 1) >= lax.broadcasted_iota(jnp.int32, (tc, tc), 0))
    for g in range(ng):
        taps_ref[...] = _mm_nn3(p_ref[g], pw_ref[g])

        def build(j, carry):
            kt = taps_ref[pl.ds(pl.multiple_of(j * S5_GROUP, S5_GROUP), S5_GROUP), :]
            for i in range(S5_GROUP):
                blk = pltpu.roll(jnp.broadcast_to(kt[i:i + 1, :], (tc, tc)), 0, 1, stride=1, stride_axis=0)
                m_ref[pl.ds(pl.multiple_of(j * tc, tc), tc), i * tc:(i + 1) * tc] = (
                    jnp.where(causal, blk, 0.0).astype(BF16))
            return carry

        lax.fori_loop(0, S5_GROUP, build, 0)
        y = jnp.dot(chunk_rows(g), m_ref[...], preferred_element_type=F32)
        y = y + jnp.dot(hp_ref[g].astype(BF16), f_ref[g], preferred_element_type=F32)
        for i in range(S5_GROUP):
            y_ref[g * S5_GROUP + i] = y[:, i * tc:(i + 1) * tc]


def _s5(ut3, p_op, pw_op, e_op, f_op, ap, nbatch, ng):
    width, r, tc = ut3.shape
    g = width // S5_GROUP
    kw = S5_GROUP * tc
    kern = functools.partial(_s5_kernel, nbatch=nbatch, ng=ng)
    blk = lambda *shape: pl.BlockSpec((ng,) + shape, lambda i: (i, 0, 0))
    chan = pl.BlockSpec((ng * S5_GROUP, r, tc), lambda i: (i, 0, 0))
    return pl.pallas_call(
        kern,
        grid=(g // ng,),
        in_specs=[chan, blk(S5_GROUP * S5_GROUP, 2 * S5_STATE), blk(2 * S5_STATE, tc),
                  blk(kw, 2 * LANES), blk(2 * LANES, kw), blk(HALO, LANES)],
        out_specs=chan,
        out_shape=jax.ShapeDtypeStruct((width, r, tc), F32),
        scratch_shapes=[pltpu.VMEM((kw, kw), BF16),
                        pltpu.VMEM((S5_GROUP * S5_GROUP, tc), F32),
                        pltpu.VMEM((ng, r, 2 * LANES), F32),
                        pltpu.VMEM((ng, r, 2 * LANES), F32)],
        compiler_params=_cparams(("parallel",)),
        name="s5",
    )(ut3, p_op, pw_op, e_op, f_op, ap)


def _s5_operators(lam_re, lam_im, log_step, b_re, b_im, c_re, c_im):
    tc, n, gs = S5_TC, S5_STATE, S5_GROUPS
    step = jnp.exp(log_step)[:, None]
    lr, li = lam_re, lam_im
    mag = jnp.exp(lr * step)
    abar_re, abar_im = mag * jnp.cos(li * step), mag * jnp.sin(li * step)
    den = lr * lr + li * li
    nr, ni = abar_re - 1.0, abar_im
    fr = (nr * lr + ni * li) / den
    fi = (ni * lr - nr * li) / den
    bb_re = fr[..., None] * b_re - fi[..., None] * b_im
    bb_im = fr[..., None] * b_im + fi[..., None] * b_re
    tau = jnp.arange(tc + 1, dtype=F32)[None, None, :]
    pmag = jnp.exp((lr * step)[..., None] * tau)
    ang = (li * step)[..., None] * tau
    p_re, p_im = pmag * jnp.cos(ang), pmag * jnp.sin(ang)
    bbt_re, bbt_im = bb_re.transpose(0, 2, 1)[:, :, None, :], bb_im.transpose(0, 2, 1)[:, :, None, :]
    cb_re = c_re[:, None] * bbt_re - c_im[:, None] * bbt_im
    cb_im = c_re[:, None] * bbt_im + c_im[:, None] * bbt_re
    p_op = jnp.concatenate([cb_re, -cb_im], axis=-1).reshape(gs, S5_GROUP * S5_GROUP, 2 * n)
    pw_op = jnp.concatenate([p_re[..., :tc], p_im[..., :tc]], axis=1)
    q_re, q_im = p_re[..., :tc][..., ::-1], p_im[..., :tc][..., ::-1]
    qs_re, qs_im = q_re.transpose(0, 2, 1)[:, None], q_im.transpose(0, 2, 1)[:, None]
    bj_re, bj_im = bb_re.transpose(0, 2, 1)[:, :, None, :], bb_im.transpose(0, 2, 1)[:, :, None, :]
    e_re = (qs_re * bj_re - qs_im * bj_im).reshape(gs, S5_GROUP * tc, n)
    e_im = (qs_re * bj_im + qs_im * bj_re).reshape(gs, S5_GROUP * tc, n)
    pad = jnp.zeros((gs, S5_GROUP * tc, LANES - n), F32)
    e_op = jnp.concatenate([e_re, pad, e_im, pad], axis=-1)
    r_re, r_im = p_re[..., 1:][:, :, None, :], p_im[..., 1:][:, :, None, :]
    ci_re, ci_im = c_re.transpose(0, 2, 1)[..., None], c_im.transpose(0, 2, 1)[..., None]
    f_re = (ci_re * r_re - ci_im * r_im).reshape(gs, n, S5_GROUP * tc)
    f_im = (ci_re * r_im + ci_im * r_re).reshape(gs, n, S5_GROUP * tc)
    padf = jnp.zeros((gs, LANES - n, S5_GROUP * tc), F32)
    f_op = jnp.concatenate([f_re, padf, -f_im, padf], axis=1)
    ap = jnp.zeros((gs, HALO, LANES), F32)
    ap = ap.at[:, 0, :n].set(p_re[..., tc]).at[:, 1, :n].set(p_im[..., tc])
    return p_op, pw_op, e_op.astype(BF16), f_op.astype(BF16), ap


def _merge_kernel(x_ref, ya_ref, yb_ref, ys_ref, u_ref, ng_ref, wg_ref, bg_ref, wb_ref, wo_ref,
                  d_ref, wglu_ref, bglu_ref, o_ref):
    x = x_ref[...]
    d = x.shape[1]
    h = _rms(x, ng_ref[...]).astype(BF16)
    yc = _gelu_tanh(ys_ref[...].T + d_ref[...] * u_ref[...])
    yc = yc * _sigmoid(_mm(yc, wglu_ref[...]) + bglu_ref[...])
    merged = None
    for i, y in enumerate((ya_ref[...], yb_ref[...], yc.astype(BF16))):
        gate = _sigmoid(jnp.dot(h, wg_ref[:, i * d:(i + 1) * d], preferred_element_type=F32)
                        + bg_ref[:, i * d:(i + 1) * d])
        term = gate * jnp.dot(y, wb_ref[i], preferred_element_type=F32)
        merged = term if merged is None else merged + term
    o_ref[...] = x + _mm(merged, wo_ref[...])


def _merge(x2, ya, yb, ys, z2, ng, wg, bg, wb, wo, d, wglu, bglu, tm):
    t, dm = x2.shape
    bw = ya.shape[1]
    row = lambda w: pl.BlockSpec((tm, w), lambda i: (i, 0))
    const2 = lambda a: pl.BlockSpec(a.shape, lambda i: (0,) * a.ndim)
    return pl.pallas_call(
        _merge_kernel,
        grid=(t // tm,),
        in_specs=[row(dm), row(bw), row(bw),
                  pl.BlockSpec((S5_WIDTH, tm), lambda i: (0, i)),
                  pl.BlockSpec((tm, S5_WIDTH), lambda i: (i, Z_S5U // S5_WIDTH)),
                  const2(ng), const2(wg), const2(bg), const2(wb), const2(wo),
                  const2(d), const2(wglu), const2(bglu)],
        out_specs=row(dm),
        out_shape=jax.ShapeDtypeStruct((t, dm), F32),
        compiler_params=_cparams(("parallel",)),
        name="merge",
    )(x2, ya, yb, ys, z2, ng, wg, bg, wb, wo, d, wglu, bglu)


def _ffn_kernel(x_ref, p_ref, ng_ref, wug_ref, wuu_ref, cw_ref, cb_ref, wd_ref, png_ref, wpg_ref,
                wpp_ref, fg_ref, o_ref, h_ref, acc_ref, ext_ref, *, tm, tiles_per_seq, final):
    i = pl.program_id(0)
    j = pl.program_id(1)
    nj = pl.num_programs(1)

    @pl.when(j == 0)
    def _():
        h_ref[...] = _rms(x_ref[...], ng_ref[...]).astype(BF16)
        acc_ref[...] = jnp.zeros(acc_ref.shape, F32)

    @pl.when(i % tiles_per_seq == 0)
    def _():
        ext_ref[j, 0:HALO, :] = jnp.zeros((HALO, ext_ref.shape[2]), F32)

    h = h_ref[...]
    ext_ref[j, HALO:HALO + tm, :] = jnp.dot(h, wug_ref[...], preferred_element_type=F32)
    u = jnp.dot(h, wuu_ref[...], preferred_element_type=F32)
    g = cb_ref[...] + cw_ref[0:1, :] * ext_ref[j, pl.ds(HALO - 2, tm), :]
    for kk in range(1, FFN_CONV):
        g = g + cw_ref[kk:kk + 1, :] * ext_ref[j, pl.ds(HALO - 2 + kk, tm), :]
    ext_ref[j, 0:HALO, :] = ext_ref[j, tm:tm + HALO, :]
    acc_ref[...] += _mm(_gelu_tanh(g) * u, wd_ref[...])

    @pl.when(j == nj - 1)
    def _():
        xn = x_ref[...] + acc_ref[...]
        gate = _sigmoid(_mm(_rms(xn, png_ref[...]), wpg_ref[...]))
        out = xn + gate * _mm(p_ref[0], wpp_ref[...])
        if final:
            out = _rms(out, fg_ref[...])
        o_ref[...] = out


def _ffn(x2, p3, layer, ng, wug, wuu, cw, cb, wd, png, wpg, wpp, fg, seq, tm, nff, final):
    t, dm = x2.shape
    dff = wd.shape[0]
    ffc = dff // nff
    kern = functools.partial(_ffn_kernel, tm=tm, tiles_per_seq=seq // tm, final=final)
    const = lambda a: pl.BlockSpec(a.shape, lambda i, j: (0,) * a.ndim)
    return pl.pallas_call(
        kern,
        grid=(t // tm, nff),
        in_specs=[pl.BlockSpec((tm, dm), lambda i, j: (i, 0)),
                  pl.BlockSpec((1, tm, p3.shape[2]), lambda i, j: (layer, i, 0)),
                  const(ng),
                  pl.BlockSpec((dm, ffc), lambda i, j: (0, j)),
                  pl.BlockSpec((dm, ffc), lambda i, j: (0, j)),
                  pl.BlockSpec((FFN_CONV, ffc), lambda i, j: (0, j)),
                  pl.BlockSpec((1, ffc), lambda i, j: (0, j)),
                  pl.BlockSpec((ffc, dm), lambda i, j: (j, 0)),
                  const(png), const(wpg), const(wpp), const(fg)],
        out_specs=pl.BlockSpec((tm, dm), lambda i, j: (i, 0)),
        out_shape=jax.ShapeDtypeStruct((t, dm), F32),
        scratch_shapes=[pltpu.VMEM((tm, dm), BF16),
                        pltpu.VMEM((tm, dm), F32),
                        pltpu.VMEM((nff, tm + HALO, ffc), F32)],
        compiler_params=_cparams(("arbitrary", "arbitrary")),
        name="ffn",
    )(x2, p3, ng, wug, wuu, cw, cb, wd, png, wpg, wpp, fg)


def _reorder_w_in(w_in):
    qk, v = DN_HEADS * DN_DK, DN_HEADS * DN_DV
    gqk, gv = GLA_HEADS * GLA_DK, GLA_HEADS * GLA_DV
    o = 0
    seg = {}
    for name, width in (("dn_q", qk), ("dn_k", qk), ("dn_v", v), ("dn_b", DN_HEADS), ("dn_a", DN_HEADS),
                        ("dn_g", v), ("gl_q", gqk), ("gl_k", gqk), ("gl_v", gv), ("gl_lr", GLA_RANK),
                        ("gl_r", gv), ("s5_u", S5_WIDTH)):
        seg[name] = w_in[:, o:o + width]
        o += width
    d = w_in.shape[0]
    small = jnp.concatenate([seg["dn_b"], seg["dn_a"], seg["gl_lr"],
                             jnp.zeros((d, LANES - 2 * DN_HEADS - GLA_RANK), w_in.dtype)], axis=1)
    return jnp.concatenate([seg["dn_q"], seg["dn_k"], seg["dn_v"], seg["dn_g"], seg["gl_q"], seg["gl_k"],
                            seg["gl_v"], seg["gl_r"], seg["s5_u"], small], axis=1)


def _lane_row(vals, offset):
    return jnp.zeros((1, LANES), F32).at[0, offset:offset + vals.shape[0]].set(vals)


def kernel(x, p, attn_norm, w_in, dn_conv_w, dn_a_log, dn_dt_bias, dn_norm, gla_w2, gla_b2, gla_norm, s5_lam_re, s5_lam_im, s5_log_step, s5_b_re, s5_b_im, s5_c_re, s5_c_im, s5_d, s5_w_glu, s5_b_glu, w_gate, b_gate, w_branch, w_o, ffn_norm, w_up, ffn_conv_w, ffn_conv_b, w_down, ple_norm, w_ple_gate, w_ple_proj, final_norm):
    bsz, seq, dm = x.shape
    depth = w_in.shape[0]
    t = bsz * seq
    dff = w_down.shape[1]
    tb = min(256, seq)
    tm = min(512, seq)
    x2 = x.reshape(t, dm)
    p3 = p.reshape(depth, t, p.shape[-1])
    row = lambda a: a.reshape(1, -1)
    for i in range(depth):
        w_in_r = _reorder_w_in(w_in[i]).astype(BF16)
        z2, ut = _inproj(x2, row(attn_norm[i]), w_in_r, tm=min(1024, t), tn=Z_WIDTH // 3)
        z3 = z2.reshape(bsz, seq, Z_WIDTH)
        ya = _deltanet(z3, dn_conv_w[i], _lane_row(dn_a_log[i], SM_A), _lane_row(dn_dt_bias[i], SM_A),
                       row(dn_norm[i]), tb)
        w2p = jnp.zeros((LANES, GLA_HEADS * GLA_DK), F32).at[SM_LR:SM_LR + GLA_RANK].set(gla_w2[i])
        yb = _gla(z3, w2p.astype(BF16), row(gla_b2[i]), row(gla_norm[i]), tb)
        s5_ops = _s5_operators(s5_lam_re[i], s5_lam_im[i], s5_log_step[i], s5_b_re[i],
                               s5_b_im[i], s5_c_re[i], s5_c_im[i])
        yt = _s5(ut.reshape(S5_WIDTH, t // S5_TC, S5_TC), *s5_ops, nbatch=bsz, ng=2)
        x2 = _merge(x2, ya.reshape(t, -1), yb.reshape(t, -1), yt.reshape(S5_WIDTH, t), z2, row(attn_norm[i]),
                    w_gate[i].astype(BF16), row(b_gate[i]), w_branch[i].astype(BF16), w_o[i].astype(BF16),
                    row(s5_d[i]), s5_w_glu[i].astype(BF16), row(s5_b_glu[i]), tm)
        x2 = _ffn(x2, p3, i, row(ffn_norm[i]), w_up[i, :, :dff].astype(BF16),
                  w_up[i, :, dff:].astype(BF16), ffn_conv_w[i], row(ffn_conv_b[i]), w_down[i].astype(BF16),
                  row(ple_norm[i]), w_ple_gate[i].astype(BF16), w_ple_proj[i].astype(BF16),
                  row(final_norm), seq, tm, 2, i == depth - 1)
    return x2.reshape(bsz, seq, dm)
```

```python
import functools
import math

import jax
import jax.numpy as jnp
from jax import lax
from jax.experimental import pallas as pl
from jax.experimental.pallas import tpu as pltpu

F32 = jnp.float32
BF16 = jnp.bfloat16

EPS = 1e-6
CHUNK = 64
DN_HEADS, DN_DK, DN_DV, DN_CONV = 4, 128, 128, 4
GLA_HEADS, GLA_DK, GLA_DV, GLA_RANK = 4, 64, 128, 16
GLA_NORMALIZER = 16.0
S5_WIDTH, S5_GROUP, S5_STATE = 512, 16, 64
S5_GROUPS = S5_WIDTH // S5_GROUP
S5_TC = 128
FFN_CONV = 3
LANES = 128
HALO = 8

Z_QKV, Z_GATE, Z_GQK, Z_GV, Z_GR, Z_S5U, Z_SMALL = 0, 1536, 2048, 2560, 3072, 3584, 4096
Z_WIDTH = 4224
SM_B, SM_A, SM_LR = 0, 4, 8

VMEM_LIMIT = 56 * 1024 * 1024


def _cparams(sem):
    return pltpu.CompilerParams(dimension_semantics=sem, vmem_limit_bytes=VMEM_LIMIT)


def _mm(a, b):
    return jnp.dot(a.astype(BF16), b.astype(BF16), preferred_element_type=F32)


def _mm_nt(a, b):
    nb = a.ndim - 2
    batch = tuple(range(nb))
    return lax.dot_general(a.astype(BF16), b.astype(BF16),
                           (((a.ndim - 1,), (b.ndim - 1,)), (batch, batch)),
                           preferred_element_type=F32)


def _mm_nn(a, b):
    nb = a.ndim - 2
    batch = tuple(range(nb))
    return lax.dot_general(a.astype(BF16), b.astype(BF16),
                           (((a.ndim - 1,), (b.ndim - 2,)), (batch, batch)),
                           preferred_element_type=F32)


def _mm_tn(a, b):
    nb = a.ndim - 2
    batch = tuple(range(nb))
    return lax.dot_general(a.astype(BF16), b.astype(BF16),
                           (((a.ndim - 2,), (b.ndim - 2,)), (batch, batch)),
                           preferred_element_type=F32)


def _split3(a):
    a1 = a.astype(BF16)
    r1 = a - a1.astype(F32)
    a2 = r1.astype(BF16)
    a3 = (r1 - a2.astype(F32)).astype(BF16)
    return a1, a2, a3


def _cumsum_rows(lmat, a):
    a1, a2, a3 = _split3(a)
    d = lambda p: jnp.dot(lmat, p, preferred_element_type=F32)
    return d(a1) + d(a2) + d(a3)


def _mm_nn3(a, b):
    a1 = a.astype(BF16)
    a2 = (a - a1.astype(F32)).astype(BF16)
    b1 = b.astype(BF16)
    b2 = (b - b1.astype(F32)).astype(BF16)
    return _mm_nn(a1, b1) + _mm_nn(a1, b2) + _mm_nn(a2, b1)


def _rms(xf, g):
    ms = jnp.mean(xf * xf, axis=-1, keepdims=True)
    return xf * lax.rsqrt(ms + EPS) * g


def _sigmoid(x):
    return 0.5 * jnp.tanh(0.5 * x) + 0.5


def _softplus(x):
    return jnp.maximum(x, 0.0) + jnp.log(1.0 + jnp.exp(-jnp.abs(x)))


def _gelu_tanh(x):
    c = math.sqrt(2.0 / math.pi)
    return 0.5 * x * (1.0 + jnp.tanh(c * (x + 0.044715 * (x * x * x))))


def _chunk_tril(tb):
    r = lax.broadcasted_iota(jnp.int32, (tb, tb), 0)
    c = lax.broadcasted_iota(jnp.int32, (tb, tb), 1)
    same = (r // CHUNK) == (c // CHUNK)
    return jnp.where(same & (c <= r), 1.0, 0.0).astype(BF16)


def _inproj_kernel(x_ref, g_ref, w_ref, o_ref, ut_ref, h_ref, *, s5_off):
    j = pl.program_id(1)

    @pl.when(j == 0)
    def _():
        h_ref[...] = _rms(x_ref[...], g_ref[...]).astype(BF16)

    o = jnp.dot(h_ref[...], w_ref[...], preferred_element_type=F32)
    o_ref[...] = o

    @pl.when(j == pl.num_programs(1) - 1)
    def _():
        ut_ref[...] = o[:, s5_off:s5_off + S5_WIDTH].T.astype(BF16)


def _inproj(x2, g, w, tm, tn):
    t, d = x2.shape
    n = w.shape[1]
    nj = n // tn
    s5_off = Z_S5U - (nj - 1) * tn
    assert 0 <= s5_off and s5_off + S5_WIDTH <= tn
    return pl.pallas_call(
        functools.partial(_inproj_kernel, s5_off=s5_off),
        grid=(t // tm, nj),
        in_specs=[pl.BlockSpec((tm, d), lambda i, j: (i, 0)),
                  pl.BlockSpec((1, d), lambda i, j: (0, 0)),
                  pl.BlockSpec((d, tn), lambda i, j: (0, j))],
        out_specs=[pl.BlockSpec((tm, tn), lambda i, j: (i, j)),
                   pl.BlockSpec((S5_WIDTH, tm), lambda i, j: (0, i))],
        out_shape=[jax.ShapeDtypeStruct((t, n), F32),
                   jax.ShapeDtypeStruct((S5_WIDTH, t), BF16)],
        scratch_shapes=[pltpu.VMEM((tm, d), BF16)],
        compiler_params=_cparams(("parallel", "arbitrary")),
        name="inproj",
    )(x2, g, w)


def _deltanet_kernel(qkv_ref, sm_ref, gate_ref, cw_ref, alog_ref, dtb_ref, ng_ref, o_ref,
                     ext_ref, s_ref, *, tb):
    nh, dk, dv = DN_HEADS, DN_DK, DN_DV
    qk_w = nh * dk

    @pl.when(pl.program_id(1) == 0)
    def _():
        ext_ref[0:HALO, :] = jnp.zeros((HALO, ext_ref.shape[1]), F32)
        s_ref[...] = jnp.zeros(s_ref.shape, F32)

    ext_ref[HALO:HALO + tb, :] = qkv_ref[0]
    ext = ext_ref[...]
    acc = cw_ref[0:1, :] * ext
    for kk in range(1, DN_CONV):
        acc = pltpu.roll(acc, 1, 0) + cw_ref[kk:kk + 1, :] * ext
    ext_ref[0:HALO, :] = ext[tb:tb + HALO, :]
    acc = acc[HALO:HALO + tb, :]
    qkv = acc * _sigmoid(acc)

    sm = sm_ref[0]
    lane = lax.broadcasted_iota(jnp.int32, (1, LANES), 1)
    coef = jnp.where((lane >= SM_A) & (lane < SM_A + nh), -jnp.exp(alog_ref[...]), 0.0)
    beta_all = _sigmoid(sm)
    glog = coef * _softplus(sm + dtb_ref[...])
    ltri = _chunk_tril(tb)
    gam_col = _cumsum_rows(ltri, glog)
    gam_row = gam_col.T

    ri = lax.broadcasted_iota(jnp.int32, (CHUNK, CHUNK), 0)
    ci = lax.broadcasted_iota(jnp.int32, (CHUNK, CHUNK), 1)
    incl = (ci <= ri)[None]
    strict = (ci < ri)[None]
    eye = jnp.where(ci == ri, 1.0, 0.0)[None]

    nc = tb // CHUNK
    pairs = [(c, h) for c in range(nc) for h in range(nh)]
    rows = lambda c: slice(c * CHUNK, (c + 1) * CHUNK)
    q = jnp.stack([qkv[rows(c), h * dk:(h + 1) * dk] for c, h in pairs])
    k = jnp.stack([qkv[rows(c), qk_w + h * dk:qk_w + (h + 1) * dk] for c, h in pairs])
    v = jnp.stack([qkv[rows(c), 2 * qk_w + h * dv:2 * qk_w + (h + 1) * dv] for c, h in pairs])
    q = q * lax.rsqrt(jnp.sum(q * q, axis=-1, keepdims=True) + EPS) * (dk ** -0.5)
    k = k * lax.rsqrt(jnp.sum(k * k, axis=-1, keepdims=True) + EPS)
    gcol = jnp.stack([gam_col[rows(c), SM_A + h:SM_A + h + 1] for c, h in pairs])
    grow = jnp.stack([gam_row[SM_A + h:SM_A + h + 1, rows(c)] for c, h in pairs])
    bcol = jnp.stack([beta_all[rows(c), SM_B + h:SM_B + h + 1] for c, h in pairs])

    dec = jnp.exp(jnp.where(incl, gcol - grow, -jnp.inf))
    kkm = _mm_nt(k, k)
    a_pos = bcol * kkm * jnp.where(strict, dec, 0.0)
    tinv = eye - a_pos
    mp = -a_pos
    for _ in range(5):
        mp = _mm_nn(mp, mp)
        tinv = tinv + _mm_nn(tinv, mp)
    eg = jnp.exp(gcol)
    rhs = jnp.concatenate([bcol * v, (bcol * eg) * k], axis=-1)
    sol = _mm_nn(tinv, rhs)
    resid = rhs - sol - _mm_nn3(a_pos, sol)
    sol = sol + _mm_nn(tinv, resid)
    u_new, w = sol[..., :dv], sol[..., dv:]
    attn = _mm_nt(q, k) * dec
    q_dec = q * eg
    gend = gcol[:, CHUNK - 1:CHUNK, :]
    k_dec = k * jnp.exp(gend - gcol)
    g_end = jnp.exp(gend)

    s_cur = s_ref[...]
    for c in range(nc):
        hs = slice(c * nh, (c + 1) * nh)
        u = u_new[hs] - _mm_nn(w[hs], s_cur)
        o = _mm_nn(q_dec[hs], s_cur) + _mm_nn(attn[hs], u)
        s_cur = g_end[hs] * s_cur + _mm_tn(k_dec[hs], u)
        o = _rms(o, ng_ref[...][None])
        for h in range(nh):
            g = gate_ref[0, rows(c), h * dv:(h + 1) * dv]
            o_ref[0, rows(c), h * dv:(h + 1) * dv] = (o[h] * (g * _sigmoid(g))).astype(o_ref.dtype)
    s_ref[...] = s_cur


def _deltanet(z3, conv_w, alog_row, dtb_row, norm_g, tb):
    b, l, _ = z3.shape
    w_qkv = 3 * DN_HEADS * DN_DK
    kern = functools.partial(_deltanet_kernel, tb=tb)
    const = lambda bi, j: (0, 0)
    return pl.pallas_call(
        kern,
        grid=(b, l // tb),
        in_specs=[pl.BlockSpec((1, tb, w_qkv), lambda bi, j: (bi, j, Z_QKV // w_qkv)),
                  pl.BlockSpec((1, tb, LANES), lambda bi, j: (bi, j, Z_SMALL // LANES)),
                  pl.BlockSpec((1, tb, 512), lambda bi, j: (bi, j, Z_GATE // 512)),
                  pl.BlockSpec((DN_CONV, w_qkv), const),
                  pl.BlockSpec((1, LANES), const),
                  pl.BlockSpec((1, LANES), const),
                  pl.BlockSpec((1, DN_DV), const)],
        out_specs=pl.BlockSpec((1, tb, DN_HEADS * DN_DV), lambda bi, j: (bi, j, 0)),
        out_shape=jax.ShapeDtypeStruct((b, l, DN_HEADS * DN_DV), BF16),
        scratch_shapes=[pltpu.VMEM((tb + HALO, w_qkv), F32),
                        pltpu.VMEM((DN_HEADS, DN_DK, DN_DV), F32)],
        compiler_params=_cparams(("parallel", "arbitrary")),
        name="deltanet",
    )(z3, z3, z3, conv_w, alog_row, dtb_row, norm_g)


def _gla_kernel(qk_ref, v_ref, r_ref, sm_ref, w2_ref, b2_ref, ng_ref, o_ref, st_ref, *, tb):
    nh, dk, dv = GLA_HEADS, GLA_DK, GLA_DV
    qk_w = nh * dk

    @pl.when(pl.program_id(1) == 0)
    def _():
        st_ref[...] = jnp.zeros(st_ref.shape, F32)

    z = _mm(sm_ref[0], w2_ref[...]) + b2_ref[...]
    glog = -_softplus(-z) * (1.0 / GLA_NORMALIZER)
    bcum = _cumsum_rows(_chunk_tril(tb), glog)
    qk = qk_ref[0]
    q = qk[:, :qk_w] * (dk ** -0.5)
    k = qk[:, qk_w:]
    q_e = q * jnp.exp(bcum)
    k_e = k * jnp.exp(-bcum)

    ri = lax.broadcasted_iota(jnp.int32, (CHUNK, CHUNK), 0)
    ci = lax.broadcasted_iota(jnp.int32, (CHUNK, CHUNK), 1)
    incl = ci <= ri

    nc = tb // CHUNK
    pairs = [(c, h) for c in range(nc) for h in range(nh)]
    rows = lambda c: slice(c * CHUNK, (c + 1) * CHUNK)
    hsl = lambda h: slice(h * dk, (h + 1) * dk)
    b_end = [bcum[(c + 1) * CHUNK - 1:(c + 1) * CHUNK, :] for c in range(nc)]
    k_dec = [k[rows(c)] * jnp.exp(b_end[c] - bcum[rows(c)]) for c in range(nc)]
    qe_p = jnp.stack([q_e[rows(c), hsl(h)] for c, h in pairs])
    ke_p = jnp.stack([k_e[rows(c), hsl(h)] for c, h in pairs])
    kd_p = jnp.stack([k_dec[c][:, hsl(h)] for c, h in pairs])
    v_p = jnp.stack([v_ref[0, rows(c), h * dv:(h + 1) * dv] for c, h in pairs])
    ge_p = jnp.stack([jnp.exp(b_end[c][:, hsl(h)]) for c, h in pairs])
    attn = jnp.where(incl[None], _mm_nt(qe_p, ke_p), 0.0)
    intra = _mm_nn(attn, v_p)
    dst = _mm_tn(v_p, kd_p)

    st = st_ref[...]
    for c in range(nc):
        ps = slice(c * nh, (c + 1) * nh)
        o = intra[ps] + _mm_nt(qe_p[ps], st)
        st = ge_p[ps] * st + dst[ps]
        o = _rms(o, ng_ref[...][None])
        for h in range(nh):
            vs = slice(h * dv, (h + 1) * dv)
            g = r_ref[0, rows(c), vs]
            o_ref[0, rows(c), vs] = (o[h] * (g * _sigmoid(g))).astype(o_ref.dtype)
    st_ref[...] = st


def _gla(z3, w2p, b2, norm_g, tb):
    b, l, _ = z3.shape
    kern = functools.partial(_gla_kernel, tb=tb)
    const = lambda bi, j: (0, 0)
    return pl.pallas_call(
        kern,
        grid=(b, l // tb),
        in_specs=[pl.BlockSpec((1, tb, 512), lambda bi, j: (bi, j, Z_GQK // 512)),
                  pl.BlockSpec((1, tb, 512), lambda bi, j: (bi, j, Z_GV // 512)),
                  pl.BlockSpec((1, tb, 512), lambda bi, j: (bi, j, Z_GR // 512)),
                  pl.BlockSpec((1, tb, LANES), lambda bi, j: (bi, j, Z_SMALL // LANES)),
                  pl.BlockSpec((LANES, GLA_HEADS * GLA_DK), const),
                  pl.BlockSpec((1, GLA_HEADS * GLA_DK), const),
                  pl.BlockSpec((1, GLA_DV), const)],
        out_specs=pl.BlockSpec((1, tb, GLA_HEADS * GLA_DV), lambda bi, j: (bi, j, 0)),
        out_shape=jax.ShapeDtypeStruct((b, l, GLA_HEADS * GLA_DV), BF16),
        scratch_shapes=[pltpu.VMEM((GLA_HEADS, GLA_DV, GLA_DK), F32)],
        compiler_params=_cparams(("parallel", "arbitrary")),
        name="gla",
    )(z3, z3, z3, z3, w2p, b2, norm_g)


def _s5_kernel(u_ref, p_ref, pw_ref, e_ref, f_ref, ap_ref, y_ref, m_ref, taps_ref, es_ref, hp_ref, *,
               nbatch, ng):
    tc = S5_TC
    nrow = u_ref.shape[1]
    nchunk = nrow // nbatch

    def chunk_rows(g):
        return jnp.concatenate([u_ref[g * S5_GROUP + j] for j in range(S5_GROUP)], axis=-1)

    for g in range(ng):
        es_ref[g] = jnp.dot(chunk_rows(g), e_ref[g], preferred_element_type=F32)

    ars = [ap_ref[g, 0:1, :] for g in range(ng)]
    ais = [ap_ref[g, 1:2, :] for g in range(ng)]

    def body(step, carry):
        new = []
        for g in range(ng):
            for b in range(nbatch):
                hr, hi = carry[2 * (g * nbatch + b)], carry[2 * (g * nbatch + b) + 1]
                base = pl.multiple_of(b * nchunk + step * HALO, HALO)
                et = es_ref[g, pl.ds(base, HALO), :]
                hrs, his = [], []
                for i in range(HALO):
                    hrs.append(hr)
                    his.append(hi)
                    er, ei = et[i:i + 1, 0:LANES], et[i:i + 1, LANES:2 * LANES]
                    hr, hi = ars[g] * hr - ais[g] * hi + er, ars[g] * hi + ais[g] * hr + ei
                hp_ref[g, pl.ds(base, HALO), 0:LANES] = jnp.concatenate(hrs, axis=0)
                hp_ref[g, pl.ds(base, HALO), LANES:2 * LANES] = jnp.concatenate(his, axis=0)
                new += [hr, hi]
        return tuple(new)

    zero = jnp.zeros((1, LANES), F32)
    lax.fori_loop(0, nchunk // HALO, body, (zero,) * (2 * nbatch * ng))

    causal = (lax.broadcasted_iota(jnp.int32, (tc, tc), 1) >= lax.broadcasted_iota(jnp.int32, (tc, tc), 0))
    for g in range(ng):
        taps_ref[...] = _mm_nn3(p_ref[g], pw_ref[g])

        def build(j, carry):
            kt = taps_ref[pl.ds(pl.multiple_of(j * S5_GROUP, S5_GROUP), S5_GROUP), :]
            for i in range(S5_GROUP):
                blk = pltpu.roll(jnp.broadcast_to(kt[i:i + 1, :], (tc, tc)), 0, 1, stride=1, stride_axis=0)
                m_ref[pl.ds(pl.multiple_of(j * tc, tc), tc), i * tc:(i + 1) * tc] = (
                    jnp.where(causal, blk, 0.0).astype(BF16))
            return carry

        lax.fori_loop(0, S5_GROUP, build, 0)
        y = jnp.dot(chunk_rows(g), m_ref[...], preferred_element_type=F32)
        y = y + jnp.dot(hp_ref[g].astype(BF16), f_ref[g], preferred_element_type=F32)
        for i in range(S5_GROUP):
            y_ref[g * S5_GROUP + i] = y[:, i * tc:(i + 1) * tc]


def _s5(ut3, p_op, pw_op, e_op, f_op, ap, nbatch, ng):
    width, r, tc = ut3.shape
    g = width // S5_GROUP
    kw = S5_GROUP * tc
    kern = functools.partial(_s5_kernel, nbatch=nbatch, ng=ng)
    blk = lambda *shape: pl.BlockSpec((ng,) + shape, lambda i: (i, 0, 0))
    chan = pl.BlockSpec((ng * S5_GROUP, r, tc), lambda i: (i, 0, 0))
    return pl.pallas_call(
        kern,
        grid=(g // ng,),
        in_specs=[chan, blk(S5_GROUP * S5_GROUP, 2 * S5_STATE), blk(2 * S5_STATE, tc),
                  blk(kw, 2 * LANES), blk(2 * LANES, kw), blk(HALO, LANES)],
        out_specs=chan,
        out_shape=jax.ShapeDtypeStruct((width, r, tc), F32),
        scratch_shapes=[pltpu.VMEM((kw, kw), BF16),
                        pltpu.VMEM((S5_GROUP * S5_GROUP, tc), F32),
                        pltpu.VMEM((ng, r, 2 * LANES), F32),
                        pltpu.VMEM((ng, r, 2 * LANES), F32)],
        compiler_params=_cparams(("parallel",)),
        name="s5",
    )(ut3, p_op, pw_op, e_op, f_op, ap)


def _s5_operators(lam_re, lam_im, log_step, b_re, b_im, c_re, c_im):
    tc, n, gs = S5_TC, S5_STATE, S5_GROUPS
    step = jnp.exp(log_step)[:, None]
    lr, li = lam_re, lam_im
    mag = jnp.exp(lr * step)
    abar_re, abar_im = mag * jnp.cos(li * step), mag * jnp.sin(li * step)
    den = lr * lr + li * li
    nr, ni = abar_re - 1.0, abar_im
    fr = (nr * lr + ni * li) / den
    fi = (ni * lr - nr * li) / den
    bb_re = fr[..., None] * b_re - fi[..., None] * b_im
    bb_im = fr[..., None] * b_im + fi[..., None] * b_re
    tau = jnp.arange(tc + 1, dtype=F32)[None, None, :]
    pmag = jnp.exp((lr * step)[..., None] * tau)
    ang = (li * step)[..., None] * tau
    p_re, p_im = pmag * jnp.cos(ang), pmag * jnp.sin(ang)
    bbt_re, bbt_im = bb_re.transpose(0, 2, 1)[:, :, None, :], bb_im.transpose(0, 2, 1)[:, :, None, :]
    cb_re = c_re[:, None] * bbt_re - c_im[:, None] * bbt_im
    cb_im = c_re[:, None] * bbt_im + c_im[:, None] * bbt_re
    p_op = jnp.concatenate([cb_re, -cb_im], axis=-1).reshape(gs, S5_GROUP * S5_GROUP, 2 * n)
    pw_op = jnp.concatenate([p_re[..., :tc], p_im[..., :tc]], axis=1)
    q_re, q_im = p_re[..., :tc][..., ::-1], p_im[..., :tc][..., ::-1]
    qs_re, qs_im = q_re.transpose(0, 2, 1)[:, None], q_im.transpose(0, 2, 1)[:, None]
    bj_re, bj_im = bb_re.transpose(0, 2, 1)[:, :, None, :], bb_im.transpose(0, 2, 1)[:, :, None, :]
    e_re = (qs_re * bj_re - qs_im * bj_im).reshape(gs, S5_GROUP * tc, n)
    e_im = (qs_re * bj_im + qs_im * bj_re).reshape(gs, S5_GROUP * tc, n)
    pad = jnp.zeros((gs, S5_GROUP * tc, LANES - n), F32)
    e_op = jnp.concatenate([e_re, pad, e_im, pad], axis=-1)
    r_re, r_im = p_re[..., 1:][:, :, None, :], p_im[..., 1:][:, :, None, :]
    ci_re, ci_im = c_re.transpose(0, 2, 1)[..., None], c_im.transpose(0, 2, 1)[..., None]
    f_re = (ci_re * r_re - ci_im * r_im).reshape(gs, n, S5_GROUP * tc)
    f_im = (ci_re * r_im + ci_im * r_re).reshape(gs, n, S5_GROUP * tc)
    padf = jnp.zeros((gs, LANES - n, S5_GROUP * tc), F32)
    f_op = jnp.concatenate([f_re, padf, -f_im, padf], axis=1)
    ap = jnp.zeros((gs, HALO, LANES), F32)
    ap = ap.at[:, 0, :n].set(p_re[..., tc]).at[:, 1, :n].set(p_im[..., tc])
    return p_op, pw_op, e_op.astype(BF16), f_op.astype(BF16), ap


def _merge_kernel(x_ref, ya_ref, yb_ref, ys_ref, u_ref, ng_ref, wg_ref, bg_ref, wb_ref, wo_ref,
                  d_ref, wglu_ref, bglu_ref, o_ref):
    x = x_ref[...]
    d = x.shape[1]
    h = _rms(x, ng_ref[...]).astype(BF16)
    yc = _gelu_tanh(ys_ref[...].T + d_ref[...] * u_ref[...])
    yc = yc * _sigmoid(_mm(yc, wglu_ref[...]) + bglu_ref[...])
    merged = None
    for i, y in enumerate((ya_ref[...], yb_ref[...], yc.astype(BF16))):
        gate = _sigmoid(jnp.dot(h, wg_ref[:, i * d:(i + 1) * d], preferred_element_type=F32)
                        + bg_ref[:, i * d:(i + 1) * d])
        term = gate * jnp.dot(y, wb_ref[i], preferred_element_type=F32)
        merged = term if merged is None else merged + term
    o_ref[...] = x + _mm(merged, wo_ref[...])


def _merge(x2, ya, yb, ys, z2, ng, wg, bg, wb, wo, d, wglu, bglu, tm):
    t, dm = x2.shape
    bw = ya.shape[1]
    row = lambda w: pl.BlockSpec((tm, w), lambda i: (i, 0))
    const2 = lambda a: pl.BlockSpec(a.shape, lambda i: (0,) * a.ndim)
    return pl.pallas_call(
        _merge_kernel,
        grid=(t // tm,),
        in_specs=[row(dm), row(bw), row(bw),
                  pl.BlockSpec((S5_WIDTH, tm), lambda i: (0, i)),
                  pl.BlockSpec((tm, S5_WIDTH), lambda i: (i, Z_S5U // S5_WIDTH)),
                  const2(ng), const2(wg), const2(bg), const2(wb), const2(wo),
                  const2(d), const2(wglu), const2(bglu)],
        out_specs=row(dm),
        out_shape=jax.ShapeDtypeStruct((t, dm), F32),
        compiler_params=_cparams(("parallel",)),
        name="merge",
    )(x2, ya, yb, ys, z2, ng, wg, bg, wb, wo, d, wglu, bglu)


def _ffn_kernel(x_ref, p_ref, ng_ref, wug_ref, wuu_ref, cw_ref, cb_ref, wd_ref, png_ref, wpg_ref,
                wpp_ref, fg_ref, o_ref, h_ref, acc_ref, ext_ref, *, tm, tiles_per_seq, final):
    i = pl.program_id(0)
    j = pl.program_id(1)
    nj = pl.num_programs(1)

    @pl.when(j == 0)
    def _():
        h_ref[...] = _rms(x_ref[...], ng_ref[...]).astype(BF16)
        acc_ref[...] = jnp.zeros(acc_ref.shape, F32)

    @pl.when(i % tiles_per_seq == 0)
    def _():
        ext_ref[j, 0:HALO, :] = jnp.zeros((HALO, ext_ref.shape[2]), F32)

    h = h_ref[...]
    ext_ref[j, HALO:HALO + tm, :] = jnp.dot(h, wug_ref[...], preferred_element_type=F32)
    u = jnp.dot(h, wuu_ref[...], preferred_element_type=F32)
    g = cb_ref[...] + cw_ref[0:1, :] * ext_ref[j, pl.ds(HALO - 2, tm), :]
    for kk in range(1, FFN_CONV):
        g = g + cw_ref[kk:kk + 1, :] * ext_ref[j, pl.ds(HALO - 2 + kk, tm), :]
    ext_ref[j, 0:HALO, :] = ext_ref[j, tm:tm + HALO, :]
    acc_ref[...] += _mm(_gelu_tanh(g) * u, wd_ref[...])

    @pl.when(j == nj - 1)
    def _():
        xn = x_ref[...] + acc_ref[...]
        gate = _sigmoid(_mm(_rms(xn, png_ref[...]), wpg_ref[...]))
        out = xn + gate * _mm(p_ref[0], wpp_ref[...])
        if final:
            out = _rms(out, fg_ref[...])
        o_ref[...] = out


def _ffn(x2, p3, layer, ng, wug, wuu, cw, cb, wd, png, wpg, wpp, fg, seq, tm, nff, final):
    t, dm = x2.shape
    dff = wd.shape[0]
    ffc = dff // nff
    kern = functools.partial(_ffn_kernel, tm=tm, tiles_per_seq=seq // tm, final=final)
    const = lambda a: pl.BlockSpec(a.shape, lambda i, j: (0,) * a.ndim)
    return pl.pallas_call(
        kern,
        grid=(t // tm, nff),
        in_specs=[pl.BlockSpec((tm, dm), lambda i, j: (i, 0)),
                  pl.BlockSpec((1, tm, p3.shape[2]), lambda i, j: (layer, i, 0)),
                  const(ng),
                  pl.BlockSpec((dm, ffc), lambda i, j: (0, j)),
                  pl.BlockSpec((dm, ffc), lambda i, j: (0, j)),
                  pl.BlockSpec((FFN_CONV, ffc), lambda i, j: (0, j)),
                  pl.BlockSpec((1, ffc), lambda i, j: (0, j)),
                  pl.BlockSpec((ffc, dm), lambda i, j: (j, 0)),
                  const(png), const(wpg), const(wpp), const(fg)],
        out_specs=pl.BlockSpec((tm, dm), lambda i, j: (i, 0)),
        out_shape=jax.ShapeDtypeStruct((t, dm), F32),
        scratch_shapes=[pltpu.VMEM((tm, dm), BF16),
                        pltpu.VMEM((tm, dm), F32),
                        pltpu.VMEM((nff, tm + HALO, ffc), F32)],
        compiler_params=_cparams(("arbitrary", "arbitrary")),
        name="ffn",
    )(x2, p3, ng, wug, wuu, cw, cb, wd, png, wpg, wpp, fg)


def _reorder_w_in(w_in):
    qk, v = DN_HEADS * DN_DK, DN_HEADS * DN_DV
    gqk, gv = GLA_HEADS * GLA_DK, GLA_HEADS * GLA_DV
    o = 0
    seg = {}
    for name, width in (("dn_q", qk), ("dn_k", qk), ("dn_v", v), ("dn_b", DN_HEADS), ("dn_a", DN_HEADS),
                        ("dn_g", v), ("gl_q", gqk), ("gl_k", gqk), ("gl_v", gv), ("gl_lr", GLA_RANK),
                        ("gl_r", gv), ("s5_u", S5_WIDTH)):
        seg[name] = w_in[:, o:o + width]
        o += width
    d = w_in.shape[0]
    small = jnp.concatenate([seg["dn_b"], seg["dn_a"], seg["gl_lr"],
                             jnp.zeros((d, LANES - 2 * DN_HEADS - GLA_RANK), w_in.dtype)], axis=1)
    return jnp.concatenate([seg["dn_q"], seg["dn_k"], seg["dn_v"], seg["dn_g"], seg["gl_q"], seg["gl_k"],
                            seg["gl_v"], seg["gl_r"], seg["s5_u"], small], axis=1)


def _lane_row(vals, offset):
    return jnp.zeros((1, LANES), F32).at[0, offset:offset + vals.shape[0]].set(vals)


def kernel(x, p, attn_norm, w_in, dn_conv_w, dn_a_log, dn_dt_bias, dn_norm, gla_w2, gla_b2, gla_norm, s5_lam_re, s5_lam_im, s5_log_step, s5_b_re, s5_b_im, s5_c_re, s5_c_im, s5_d, s5_w_glu, s5_b_glu, w_gate, b_gate, w_branch, w_o, ffn_norm, w_up, ffn_conv_w, ffn_conv_b, w_down, ple_norm, w_ple_gate, w_ple_proj, final_norm):
    bsz, seq, dm = x.shape
    depth = w_in.shape[0]
    t = bsz * seq
    dff = w_down.shape[1]
    tb = min(512, seq)
    tm = min(512, seq)
    x2 = x.reshape(t, dm)
    p3 = p.reshape(depth, t, p.shape[-1])
    row = lambda a: a.reshape(1, -1)
    for i in range(depth):
        w_in_r = _reorder_w_in(w_in[i]).astype(BF16)
        z2, ut = _inproj(x2, row(attn_norm[i]), w_in_r, tm=min(1024, t), tn=Z_WIDTH // 3)
        z3 = z2.reshape(bsz, seq, Z_WIDTH)
        ya = _deltanet(z3, dn_conv_w[i], _lane_row(dn_a_log[i], SM_A), _lane_row(dn_dt_bias[i], SM_A),
                       row(dn_norm[i]), tb)
        w2p = jnp.zeros((LANES, GLA_HEADS * GLA_DK), F32).at[SM_LR:SM_LR + GLA_RANK].set(gla_w2[i])
        yb = _gla(z3, w2p.astype(BF16), row(gla_b2[i]), row(gla_norm[i]), tb)
        s5_ops = _s5_operators(s5_lam_re[i], s5_lam_im[i], s5_log_step[i], s5_b_re[i],
                               s5_b_im[i], s5_c_re[i], s5_c_im[i])
        yt = _s5(ut.reshape(S5_WIDTH, t // S5_TC, S5_TC), *s5_ops, nbatch=bsz, ng=2)
        x2 = _merge(x2, ya.reshape(t, -1), yb.reshape(t, -1), yt.reshape(S5_WIDTH, t), z2, row(attn_norm[i]),
                    w_gate[i].astype(BF16), row(b_gate[i]), w_branch[i].astype(BF16), w_o[i].astype(BF16),
                    row(s5_d[i]), s5_w_glu[i].astype(BF16), row(s5_b_glu[i]), tm)
        x2 = _ffn(x2, p3, i, row(ffn_norm[i]), w_up[i, :, :dff].astype(BF16),
                  w_up[i, :, dff:].astype(BF16), ffn_conv_w[i], row(ffn_conv_b[i]), w_down[i].astype(BF16),
                  row(ple_norm[i]), w_ple_gate[i].astype(BF16), w_ple_proj[i].astype(BF16),
                  row(final_norm), seq, tm, 2, i == depth - 1)
    return x2.reshape(bsz, seq, dm)
```

```python
import functools
import math

import jax
import jax.numpy as jnp
from jax import lax
from jax.experimental import pallas as pl
from jax.experimental.pallas import tpu as pltpu

F32 = jnp.float32
BF16 = jnp.bfloat16

EPS = 1e-6
CHUNK = 64
DN_HEADS, DN_DK, DN_DV, DN_CONV = 4, 128, 128, 4
GLA_HEADS, GLA_DK, GLA_DV, GLA_RANK = 4, 64, 128, 16
GLA_NORMALIZER = 16.0
S5_WIDTH, S5_GROUP, S5_STATE = 512, 16, 64
S5_GROUPS = S5_WIDTH // S5_GROUP
S5_TC = 128
FFN_CONV = 3
LANES = 128
HALO = 8

Z_QKV, Z_GATE, Z_GQK, Z_GV, Z_GR, Z_S5U, Z_SMALL = 0, 1536, 2048, 2560, 3072, 3584, 4096
Z_WIDTH = 4224
SM_B, SM_A, SM_LR = 0, 4, 8

VMEM_LIMIT = 56 * 1024 * 1024


def _cparams(sem):
    return pltpu.CompilerParams(dimension_semantics=sem, vmem_limit_bytes=VMEM_LIMIT)


def _mm(a, b):
    return jnp.dot(a.astype(BF16), b.astype(BF16), preferred_element_type=F32)


def _mm_nt(a, b):
    nb = a.ndim - 2
    batch = tuple(range(nb))
    return lax.dot_general(a.astype(BF16), b.astype(BF16),
                           (((a.ndim - 1,), (b.ndim - 1,)), (batch, batch)),
                           preferred_element_type=F32)


def _mm_nn(a, b):
    nb = a.ndim - 2
    batch = tuple(range(nb))
    return lax.dot_general(a.astype(BF16), b.astype(BF16),
                           (((a.ndim - 1,), (b.ndim - 2,)), (batch, batch)),
                           preferred_element_type=F32)


def _mm_tn(a, b):
    nb = a.ndim - 2
    batch = tuple(range(nb))
    return lax.dot_general(a.astype(BF16), b.astype(BF16),
                           (((a.ndim - 2,), (b.ndim - 2,)), (batch, batch)),
                           preferred_element_type=F32)


def _split3(a):
    a1 = a.astype(BF16)
    r1 = a - a1.astype(F32)
    a2 = r1.astype(BF16)
    a3 = (r1 - a2.astype(F32)).astype(BF16)
    return a1, a2, a3


def _cumsum_rows(lmat, a):
    a1, a2, a3 = _split3(a)
    d = lambda p: jnp.dot(lmat, p, preferred_element_type=F32)
    return d(a1) + d(a2) + d(a3)


def _mm_nn3(a, b):
    a1 = a.astype(BF16)
    a2 = (a - a1.astype(F32)).astype(BF16)
    b1 = b.astype(BF16)
    b2 = (b - b1.astype(F32)).astype(BF16)
    return _mm_nn(a1, b1) + _mm_nn(a1, b2) + _mm_nn(a2, b1)


def _rms(xf, g):
    ms = jnp.mean(xf * xf, axis=-1, keepdims=True)
    return xf * lax.rsqrt(ms + EPS) * g


def _sigmoid(x):
    return 0.5 * jnp.tanh(0.5 * x) + 0.5


def _softplus(x):
    return jnp.maximum(x, 0.0) + jnp.log(1.0 + jnp.exp(-jnp.abs(x)))


def _gelu_tanh(x):
    c = math.sqrt(2.0 / math.pi)
    return 0.5 * x * (1.0 + jnp.tanh(c * (x + 0.044715 * (x * x * x))))


def _chunk_tril(tb):
    r = lax.broadcasted_iota(jnp.int32, (tb, tb), 0)
    c = lax.broadcasted_iota(jnp.int32, (tb, tb), 1)
    same = (r // CHUNK) == (c // CHUNK)
    return jnp.where(same & (c <= r), 1.0, 0.0).astype(BF16)


def _inproj_kernel(x_ref, g_ref, w_ref, o_ref, sm_ref, ut_ref, h_ref, *, s5_off, sm_off):
    j = pl.program_id(1)

    @pl.when(j == 0)
    def _():
        h_ref[...] = _rms(x_ref[...], g_ref[...]).astype(BF16)

    o = jnp.dot(h_ref[...], w_ref[...], preferred_element_type=F32)
    o_ref[...] = o.astype(o_ref.dtype)

    @pl.when(j == pl.num_programs(1) - 1)
    def _():
        sm_ref[...] = o[:, sm_off:sm_off + LANES]
        ut_ref[...] = o[:, s5_off:s5_off + S5_WIDTH].T.astype(BF16)


def _inproj(x2, g, w, tm, tn):
    t, d = x2.shape
    n = w.shape[1]
    nj = n // tn
    s5_off = Z_S5U - (nj - 1) * tn
    sm_off = Z_SMALL - (nj - 1) * tn
    assert 0 <= s5_off and sm_off + LANES <= tn
    return pl.pallas_call(
        functools.partial(_inproj_kernel, s5_off=s5_off, sm_off=sm_off),
        grid=(t // tm, nj),
        in_specs=[pl.BlockSpec((tm, d), lambda i, j: (i, 0)),
                  pl.BlockSpec((1, d), lambda i, j: (0, 0)),
                  pl.BlockSpec((d, tn), lambda i, j: (0, j))],
        out_specs=[pl.BlockSpec((tm, tn), lambda i, j: (i, j)),
                   pl.BlockSpec((tm, LANES), lambda i, j: (i, 0)),
                   pl.BlockSpec((S5_WIDTH, tm), lambda i, j: (0, i))],
        out_shape=[jax.ShapeDtypeStruct((t, n), BF16),
                   jax.ShapeDtypeStruct((t, LANES), F32),
                   jax.ShapeDtypeStruct((S5_WIDTH, t), BF16)],
        scratch_shapes=[pltpu.VMEM((tm, d), BF16)],
        compiler_params=_cparams(("parallel", "arbitrary")),
        name="inproj",
    )(x2, g, w)


def _deltanet_kernel(qkv_ref, sm_ref, gate_ref, cw_ref, alog_ref, dtb_ref, ng_ref, o_ref,
                     ext_ref, s_ref, *, tb):
    nh, dk, dv = DN_HEADS, DN_DK, DN_DV
    qk_w = nh * dk

    @pl.when(pl.program_id(1) == 0)
    def _():
        ext_ref[0:HALO, :] = jnp.zeros((HALO, ext_ref.shape[1]), F32)
        s_ref[...] = jnp.zeros(s_ref.shape, F32)

    ext_ref[HALO:HALO + tb, :] = qkv_ref[0].astype(F32)
    ext = ext_ref[...]
    acc = cw_ref[0:1, :] * ext
    for kk in range(1, DN_CONV):
        acc = pltpu.roll(acc, 1, 0) + cw_ref[kk:kk + 1, :] * ext
    ext_ref[0:HALO, :] = ext[tb:tb + HALO, :]
    acc = acc[HALO:HALO + tb, :]
    qkv = acc * _sigmoid(acc)

    sm = sm_ref[0]
    lane = lax.broadcasted_iota(jnp.int32, (1, LANES), 1)
    coef = jnp.where((lane >= SM_A) & (lane < SM_A + nh), -jnp.exp(alog_ref[...]), 0.0)
    beta_all = _sigmoid(sm)
    glog = coef * _softplus(sm + dtb_ref[...])
    ltri = _chunk_tril(tb)
    gam_col = _cumsum_rows(ltri, glog)
    gam_row = gam_col.T

    ri = lax.broadcasted_iota(jnp.int32, (CHUNK, CHUNK), 0)
    ci = lax.broadcasted_iota(jnp.int32, (CHUNK, CHUNK), 1)
    incl = (ci <= ri)[None]
    strict = (ci < ri)[None]
    eye = jnp.where(ci == ri, 1.0, 0.0)[None]

    nc = tb // CHUNK
    pairs = [(c, h) for c in range(nc) for h in range(nh)]
    rows = lambda c: slice(c * CHUNK, (c + 1) * CHUNK)
    q = jnp.stack([qkv[rows(c), h * dk:(h + 1) * dk] for c, h in pairs])
    k = jnp.stack([qkv[rows(c), qk_w + h * dk:qk_w + (h + 1) * dk] for c, h in pairs])
    v = jnp.stack([qkv[rows(c), 2 * qk_w + h * dv:2 * qk_w + (h + 1) * dv] for c, h in pairs])
    q = q * lax.rsqrt(jnp.sum(q * q, axis=-1, keepdims=True) + EPS) * (dk ** -0.5)
    k = k * lax.rsqrt(jnp.sum(k * k, axis=-1, keepdims=True) + EPS)
    gcol = jnp.stack([gam_col[rows(c), SM_A + h:SM_A + h + 1] for c, h in pairs])
    grow = jnp.stack([gam_row[SM_A + h:SM_A + h + 1, rows(c)] for c, h in pairs])
    bcol = jnp.stack([beta_all[rows(c), SM_B + h:SM_B + h + 1] for c, h in pairs])

    dec = jnp.exp(jnp.where(incl, gcol - grow, -jnp.inf))
    kkm = _mm_nt(k, k)
    a_pos = bcol * kkm * jnp.where(strict, dec, 0.0)
    tinv = eye - a_pos
    mp = -a_pos
    for _ in range(5):
        mp = _mm_nn(mp, mp)
        tinv = tinv + _mm_nn(tinv, mp)
    eg = jnp.exp(gcol)
    rhs = jnp.concatenate([bcol * v, (bcol * eg) * k], axis=-1)
    sol = _mm_nn(tinv, rhs)
    resid = rhs - sol - _mm_nn3(a_pos, sol)
    sol = sol + _mm_nn(tinv, resid)
    u_new, w = sol[..., :dv], sol[..., dv:]
    attn = _mm_nt(q, k) * dec
    q_dec = q * eg
    gend = gcol[:, CHUNK - 1:CHUNK, :]
    k_dec = k * jnp.exp(gend - gcol)
    g_end = jnp.exp(gend)

    s_cur = s_ref[...]
    for c in range(nc):
        hs = slice(c * nh, (c + 1) * nh)
        u = u_new[hs] - _mm_nn(w[hs], s_cur)
        o = _mm_nn(q_dec[hs], s_cur) + _mm_nn(attn[hs], u)
        s_cur = g_end[hs] * s_cur + _mm_tn(k_dec[hs], u)
        o = _rms(o, ng_ref[...][None])
        for h in range(nh):
            g = gate_ref[0, rows(c), h * dv:(h + 1) * dv].astype(F32)
            o_ref[0, rows(c), h * dv:(h + 1) * dv] = (o[h] * (g * _sigmoid(g))).astype(o_ref.dtype)
    s_ref[...] = s_cur


def _deltanet(z3, sm3, conv_w, alog_row, dtb_row, norm_g, tb):
    b, l, _ = z3.shape
    w_qkv = 3 * DN_HEADS * DN_DK
    kern = functools.partial(_deltanet_kernel, tb=tb)
    const = lambda bi, j: (0, 0)
    return pl.pallas_call(
        kern,
        grid=(b, l // tb),
        in_specs=[pl.BlockSpec((1, tb, w_qkv), lambda bi, j: (bi, j, Z_QKV // w_qkv)),
                  pl.BlockSpec((1, tb, LANES), lambda bi, j: (bi, j, 0)),
                  pl.BlockSpec((1, tb, 512), lambda bi, j: (bi, j, Z_GATE // 512)),
                  pl.BlockSpec((DN_CONV, w_qkv), const),
                  pl.BlockSpec((1, LANES), const),
                  pl.BlockSpec((1, LANES), const),
                  pl.BlockSpec((1, DN_DV), const)],
        out_specs=pl.BlockSpec((1, tb, DN_HEADS * DN_DV), lambda bi, j: (bi, j, 0)),
        out_shape=jax.ShapeDtypeStruct((b, l, DN_HEADS * DN_DV), BF16),
        scratch_shapes=[pltpu.VMEM((tb + HALO, w_qkv), F32),
                        pltpu.VMEM((DN_HEADS, DN_DK, DN_DV), F32)],
        compiler_params=_cparams(("parallel", "arbitrary")),
        name="deltanet",
    )(z3, sm3, z3, conv_w, alog_row, dtb_row, norm_g)


def _gla_kernel(qk_ref, v_ref, r_ref, sm_ref, w2_ref, b2_ref, ng_ref, o_ref, st_ref, *, tb):
    nh, dk, dv = GLA_HEADS, GLA_DK, GLA_DV
    qk_w = nh * dk

    @pl.when(pl.program_id(1) == 0)
    def _():
        st_ref[...] = jnp.zeros(st_ref.shape, F32)

    z = _mm(sm_ref[0], w2_ref[...]) + b2_ref[...]
    glog = -_softplus(-z) * (1.0 / GLA_NORMALIZER)
    bcum = _cumsum_rows(_chunk_tril(tb), glog)
    qk = qk_ref[0].astype(F32)
    q = qk[:, :qk_w] * (dk ** -0.5)
    k = qk[:, qk_w:]
    q_e = q * jnp.exp(bcum)
    k_e = k * jnp.exp(-bcum)

    ri = lax.broadcasted_iota(jnp.int32, (CHUNK, CHUNK), 0)
    ci = lax.broadcasted_iota(jnp.int32, (CHUNK, CHUNK), 1)
    incl = ci <= ri

    nc = tb // CHUNK
    pairs = [(c, h) for c in range(nc) for h in range(nh)]
    rows = lambda c: slice(c * CHUNK, (c + 1) * CHUNK)
    hsl = lambda h: slice(h * dk, (h + 1) * dk)
    b_end = [bcum[(c + 1) * CHUNK - 1:(c + 1) * CHUNK, :] for c in range(nc)]
    k_dec = [k[rows(c)] * jnp.exp(b_end[c] - bcum[rows(c)]) for c in range(nc)]
    qe_p = jnp.stack([q_e[rows(c), hsl(h)] for c, h in pairs])
    ke_p = jnp.stack([k_e[rows(c), hsl(h)] for c, h in pairs])
    kd_p = jnp.stack([k_dec[c][:, hsl(h)] for c, h in pairs])
    v_p = jnp.stack([v_ref[0, rows(c), h * dv:(h + 1) * dv] for c, h in pairs])
    ge_p = jnp.stack([jnp.exp(b_end[c][:, hsl(h)]) for c, h in pairs])
    attn = jnp.where(incl[None], _mm_nt(qe_p, ke_p), 0.0)
    intra = _mm_nn(attn, v_p)
    dst = _mm_tn(v_p, kd_p)

    st = st_ref[...]
    for c in range(nc):
        ps = slice(c * nh, (c + 1) * nh)
        o = intra[ps] + _mm_nt(qe_p[ps], st)
        st = ge_p[ps] * st + dst[ps]
        o = _rms(o, ng_ref[...][None])
        for h in range(nh):
            vs = slice(h * dv, (h + 1) * dv)
            g = r_ref[0, rows(c), vs].astype(F32)
            o_ref[0, rows(c), vs] = (o[h] * (g * _sigmoid(g))).astype(o_ref.dtype)
    st_ref[...] = st


def _gla(z3, sm3, w2p, b2, norm_g, tb):
    b, l, _ = z3.shape
    kern = functools.partial(_gla_kernel, tb=tb)
    const = lambda bi, j: (0, 0)
    return pl.pallas_call(
        kern,
        grid=(b, l // tb),
        in_specs=[pl.BlockSpec((1, tb, 512), lambda bi, j: (bi, j, Z_GQK // 512)),
                  pl.BlockSpec((1, tb, 512), lambda bi, j: (bi, j, Z_GV // 512)),
                  pl.BlockSpec((1, tb, 512), lambda bi, j: (bi, j, Z_GR // 512)),
                  pl.BlockSpec((1, tb, LANES), lambda bi, j: (bi, j, 0)),
                  pl.BlockSpec((LANES, GLA_HEADS * GLA_DK), const),
                  pl.BlockSpec((1, GLA_HEADS * GLA_DK), const),
                  pl.BlockSpec((1, GLA_DV), const)],
        out_specs=pl.BlockSpec((1, tb, GLA_HEADS * GLA_DV), lambda bi, j: (bi, j, 0)),
        out_shape=jax.ShapeDtypeStruct((b, l, GLA_HEADS * GLA_DV), BF16),
        scratch_shapes=[pltpu.VMEM((GLA_HEADS, GLA_DV, GLA_DK), F32)],
        compiler_params=_cparams(("parallel", "arbitrary")),
        name="gla",
    )(z3, z3, z3, sm3, w2p, b2, norm_g)


def _s5_kernel(u_ref, p_ref, pw_ref, e_ref, f_ref, ap_ref, y_ref, m_ref, taps_ref, es_ref, hp_ref, *,
               nbatch, ng):
    tc = S5_TC
    nrow = u_ref.shape[1]
    nchunk = nrow // nbatch

    def chunk_rows(g):
        return jnp.concatenate([u_ref[g * S5_GROUP + j] for j in range(S5_GROUP)], axis=-1)

    for g in range(ng):
        es_ref[g] = jnp.dot(chunk_rows(g), e_ref[g], preferred_element_type=F32)

    ars = [ap_ref[g, 0:1, :] for g in range(ng)]
    ais = [ap_ref[g, 1:2, :] for g in range(ng)]

    def body(step, carry):
        new = []
        for g in range(ng):
            for b in range(nbatch):
                hr, hi = carry[2 * (g * nbatch + b)], carry[2 * (g * nbatch + b) + 1]
                base = pl.multiple_of(b * nchunk + step * HALO, HALO)
                et = es_ref[g, pl.ds(base, HALO), :]
                hrs, his = [], []
                for i in range(HALO):
                    hrs.append(hr)
                    his.append(hi)
                    er, ei = et[i:i + 1, 0:LANES], et[i:i + 1, LANES:2 * LANES]
                    hr, hi = ars[g] * hr - ais[g] * hi + er, ars[g] * hi + ais[g] * hr + ei
                hp_ref[g, pl.ds(base, HALO), 0:LANES] = jnp.concatenate(hrs, axis=0)
                hp_ref[g, pl.ds(base, HALO), LANES:2 * LANES] = jnp.concatenate(his, axis=0)
                new += [hr, hi]
        return tuple(new)

    zero = jnp.zeros((1, LANES), F32)
    lax.fori_loop(0, nchunk // HALO, body, (zero,) * (2 * nbatch * ng))

    causal = (lax.broadcasted_iota(jnp.int32, (tc, tc), 1) >= lax.broadcasted_iota(jnp.int32, (tc, tc), 0))
    for g in range(ng):
        taps_ref[...] = _mm_nn3(p_ref[g], pw_ref[g])

        def build(j, carry):
            kt = taps_ref[pl.ds(pl.multiple_of(j * S5_GROUP, S5_GROUP), S5_GROUP), :]
            for i in range(S5_GROUP):
                blk = pltpu.roll(jnp.broadcast_to(kt[i:i + 1, :], (tc, tc)), 0, 1, stride=1, stride_axis=0)
                m_ref[pl.ds(pl.multiple_of(j * tc, tc), tc), i * tc:(i + 1) * tc] = (
                    jnp.where(causal, blk, 0.0).astype(BF16))
            return carry

        lax.fori_loop(0, S5_GROUP, build, 0)
        y = jnp.dot(chunk_rows(g), m_ref[...], preferred_element_type=F32)
        y = y + jnp.dot(hp_ref[g].astype(BF16), f_ref[g], preferred_element_type=F32)
        for i in range(S5_GROUP):
            y_ref[g * S5_GROUP + i] = y[:, i * tc:(i + 1) * tc]


def _s5(ut3, p_op, pw_op, e_op, f_op, ap, nbatch, ng):
    width, r, tc = ut3.shape
    g = width // S5_GROUP
    kw = S5_GROUP * tc
    kern = functools.partial(_s5_kernel, nbatch=nbatch, ng=ng)
    blk = lambda *shape: pl.BlockSpec((ng,) + shape, lambda i: (i, 0, 0))
    chan = pl.BlockSpec((ng * S5_GROUP, r, tc), lambda i: (i, 0, 0))
    return pl.pallas_call(
        kern,
        grid=(g // ng,),
        in_specs=[chan, blk(S5_GROUP * S5_GROUP, 2 * S5_STATE), blk(2 * S5_STATE, tc),
                  blk(kw, 2 * LANES), blk(2 * LANES, kw), blk(HALO, LANES)],
        out_specs=chan,
        out_shape=jax.ShapeDtypeStruct((width, r, tc), F32),
        scratch_shapes=[pltpu.VMEM((kw, kw), BF16),
                        pltpu.VMEM((S5_GROUP * S5_GROUP, tc), F32),
                        pltpu.VMEM((ng, r, 2 * LANES), F32),
                        pltpu.VMEM((ng, r, 2 * LANES), F32)],
        compiler_params=_cparams(("parallel",)),
        name="s5",
    )(ut3, p_op, pw_op, e_op, f_op, ap)


def _s5_operators(lam_re, lam_im, log_step, b_re, b_im, c_re, c_im):
    tc, n, gs = S5_TC, S5_STATE, S5_GROUPS
    step = jnp.exp(log_step)[:, None]
    lr, li = lam_re, lam_im
    mag = jnp.exp(lr * step)
    abar_re, abar_im = mag * jnp.cos(li * step), mag * jnp.sin(li * step)
    den = lr * lr + li * li
    nr, ni = abar_re - 1.0, abar_im
    fr = (nr * lr + ni * li) / den
    fi = (ni * lr - nr * li) / den
    bb_re = fr[..., None] * b_re - fi[..., None] * b_im
    bb_im = fr[..., None] * b_im + fi[..., None] * b_re
    tau = jnp.arange(tc + 1, dtype=F32)[None, None, :]
    pmag = jnp.exp((lr * step)[..., None] * tau)
    ang = (li * step)[..., None] * tau
    p_re, p_im = pmag * jnp.cos(ang), pmag * jnp.sin(ang)
    bbt_re, bbt_im = bb_re.transpose(0, 2, 1)[:, :, None, :], bb_im.transpose(0, 2, 1)[:, :, None, :]
    cb_re = c_re[:, None] * bbt_re - c_im[:, None] * bbt_im
    cb_im = c_re[:, None] * bbt_im + c_im[:, None] * bbt_re
    p_op = jnp.concatenate([cb_re, -cb_im], axis=-1).reshape(gs, S5_GROUP * S5_GROUP, 2 * n)
    pw_op = jnp.concatenate([p_re[..., :tc], p_im[..., :tc]], axis=1)
    q_re, q_im = p_re[..., :tc][..., ::-1], p_im[..., :tc][..., ::-1]
    qs_re, qs_im = q_re.transpose(0, 2, 1)[:, None], q_im.transpose(0, 2, 1)[:, None]
    bj_re, bj_im = bb_re.transpose(0, 2, 1)[:, :, None, :], bb_im.transpose(0, 2, 1)[:, :, None, :]
    e_re = (qs_re * bj_re - qs_im * bj_im).reshape(gs, S5_GROUP * tc, n)
    e_im = (qs_re * bj_im + qs_im * bj_re).reshape(gs, S5_GROUP * tc, n)
    pad = jnp.zeros((gs, S5_GROUP * tc, LANES - n), F32)
    e_op = jnp.concatenate([e_re, pad, e_im, pad], axis=-1)
    r_re, r_im = p_re[..., 1:][:, :, None, :], p_im[..., 1:][:, :, None, :]
    ci_re, ci_im = c_re.transpose(0, 2, 1)[..., None], c_im.transpose(0, 2, 1)[..., None]
    f_re = (ci_re * r_re - ci_im * r_im).reshape(gs, n, S5_GROUP * tc)
    f_im = (ci_re * r_im + ci_im * r_re).reshape(gs, n, S5_GROUP * tc)
    padf = jnp.zeros((gs, LANES - n, S5_GROUP * tc), F32)
    f_op = jnp.concatenate([f_re, padf, -f_im, padf], axis=1)
    ap = jnp.zeros((gs, HALO, LANES), F32)
    ap = ap.at[:, 0, :n].set(p_re[..., tc]).at[:, 1, :n].set(p_im[..., tc])
    return p_op, pw_op, e_op.astype(BF16), f_op.astype(BF16), ap


def _merge_kernel(x_ref, ya_ref, yb_ref, ys_ref, u_ref, ng_ref, wg_ref, bg_ref, wb_ref, wo_ref,
                  d_ref, wglu_ref, bglu_ref, o_ref):
    x = x_ref[...]
    d = x.shape[1]
    h = _rms(x, ng_ref[...]).astype(BF16)
    yc = _gelu_tanh(ys_ref[...].T + d_ref[...] * u_ref[...].astype(F32))
    yc = yc * _sigmoid(_mm(yc, wglu_ref[...]) + bglu_ref[...])
    merged = None
    for i, y in enumerate((ya_ref[...], yb_ref[...], yc.astype(BF16))):
        gate = _sigmoid(jnp.dot(h, wg_ref[:, i * d:(i + 1) * d], preferred_element_type=F32)
                        + bg_ref[:, i * d:(i + 1) * d])
        term = gate * jnp.dot(y, wb_ref[i], preferred_element_type=F32)
        merged = term if merged is None else merged + term
    o_ref[...] = x + _mm(merged, wo_ref[...])


def _merge(x2, ya, yb, ys, z2, ng, wg, bg, wb, wo, d, wglu, bglu, tm):
    t, dm = x2.shape
    bw = ya.shape[1]
    row = lambda w: pl.BlockSpec((tm, w), lambda i: (i, 0))
    const2 = lambda a: pl.BlockSpec(a.shape, lambda i: (0,) * a.ndim)
    return pl.pallas_call(
        _merge_kernel,
        grid=(t // tm,),
        in_specs=[row(dm), row(bw), row(bw),
                  pl.BlockSpec((S5_WIDTH, tm), lambda i: (0, i)),
                  pl.BlockSpec((tm, S5_WIDTH), lambda i: (i, Z_S5U // S5_WIDTH)),
                  const2(ng), const2(wg), const2(bg), const2(wb), const2(wo),
                  const2(d), const2(wglu), const2(bglu)],
        out_specs=row(dm),
        out_shape=jax.ShapeDtypeStruct((t, dm), F32),
        compiler_params=_cparams(("parallel",)),
        name="merge",
    )(x2, ya, yb, ys, z2, ng, wg, bg, wb, wo, d, wglu, bglu)


def _ffn_kernel(x_ref, p_ref, ng_ref, wu_ref, cw_ref, cb_ref, wd_ref, png_ref, wpg_ref,
                wpp_ref, fg_ref, o_ref, tail_ref, act_ref, *, tm, tiles_per_seq, ffc, down_group, final):
    dff = wd_ref.shape[0]

    @pl.when(pl.program_id(0) % tiles_per_seq == 0)
    def _():
        tail_ref[...] = jnp.zeros(tail_ref.shape, F32)

    x = x_ref[...]
    h = _rms(x, ng_ref[...]).astype(BF16)

    def up(c0):
        return (jnp.dot(h, wu_ref[:, c0:c0 + ffc], preferred_element_type=F32),
                jnp.dot(h, wu_ref[:, dff + c0:dff + c0 + ffc], preferred_element_type=F32))

    starts = list(range(0, dff, ffc))
    acc = None
    group0 = 0
    gu = up(0)
    for idx, c0 in enumerate(starts):
        cols = slice(c0, c0 + ffc)
        g, u = gu
        if idx + 1 < len(starts):
            gu = up(starts[idx + 1])
        ext = jnp.concatenate([tail_ref[:, cols], g], axis=0)
        tail_ref[:, cols] = g[tm - HALO:tm, :]
        cv = cw_ref[0:1, cols] * ext
        for kk in range(1, FFN_CONV):
            cv = pltpu.roll(cv, 1, 0) + cw_ref[kk:kk + 1, cols] * ext
        act_ref[:, cols] = (_gelu_tanh(cv[HALO:HALO + tm, :] + cb_ref[:, cols]) * u).astype(BF16)
        if (idx + 1) % down_group == 0 or idx + 1 == len(starts):
            grp = slice(group0, c0 + ffc)
            part = jnp.dot(act_ref[:, grp], wd_ref[grp, :], preferred_element_type=F32)
            acc = part if acc is None else acc + part
            group0 = c0 + ffc

    xn = x + acc
    gate = _sigmoid(_mm(_rms(xn, png_ref[...]), wpg_ref[...]))
    out = xn + gate * _mm(p_ref[0], wpp_ref[...])
    if final:
        out = _rms(out, fg_ref[...])
    o_ref[...] = out


def _ffn(x2, p3, layer, ng, wu, cw, cb, wd, png, wpg, wpp, fg, seq, tm, ffc, final):
    t, dm = x2.shape
    dff = wd.shape[0]
    assert dff % ffc == 0 and ffc % LANES == 0
    kern = functools.partial(_ffn_kernel, tm=tm, tiles_per_seq=seq // tm, ffc=ffc, down_group=4, final=final)
    const = lambda a: pl.BlockSpec(a.shape, lambda i: (0,) * a.ndim, pipeline_mode=pl.Buffered(1))
    return pl.pallas_call(
        kern,
        grid=(t // tm,),
        in_specs=[pl.BlockSpec((tm, dm), lambda i: (i, 0)),
                  pl.BlockSpec((1, tm, p3.shape[2]), lambda i: (layer, i, 0)),
                  const(ng), const(wu), const(cw), const(cb), const(wd),
                  const(png), const(wpg), const(wpp), const(fg)],
        out_specs=pl.BlockSpec((tm, dm), lambda i: (i, 0)),
        out_shape=jax.ShapeDtypeStruct((t, dm), F32),
        scratch_shapes=[pltpu.VMEM((HALO, dff), F32), pltpu.VMEM((tm, dff), BF16)],
        compiler_params=_cparams(("arbitrary",)),
        name="ffn",
    )(x2, p3, ng, wu, cw, cb, wd, png, wpg, wpp, fg)


def _reorder_w_in(w_in):
    qk, v = DN_HEADS * DN_DK, DN_HEADS * DN_DV
    gqk, gv = GLA_HEADS * GLA_DK, GLA_HEADS * GLA_DV
    o = 0
    seg = {}
    for name, width in (("dn_q", qk), ("dn_k", qk), ("dn_v", v), ("dn_b", DN_HEADS), ("dn_a", DN_HEADS),
                        ("dn_g", v), ("gl_q", gqk), ("gl_k", gqk), ("gl_v", gv), ("gl_lr", GLA_RANK),
                        ("gl_r", gv), ("s5_u", S5_WIDTH)):
        seg[name] = w_in[:, o:o + width]
        o += width
    d = w_in.shape[0]
    small = jnp.concatenate([seg["dn_b"], seg["dn_a"], seg["gl_lr"],
                             jnp.zeros((d, LANES - 2 * DN_HEADS - GLA_RANK), w_in.dtype)], axis=1)
    return jnp.concatenate([seg["dn_q"], seg["dn_k"], seg["dn_v"], seg["dn_g"], seg["gl_q"], seg["gl_k"],
                            seg["gl_v"], seg["gl_r"], seg["s5_u"], small], axis=1)


def _lane_row(vals, offset):
    return jnp.zeros((1, LANES), F32).at[0, offset:offset + vals.shape[0]].set(vals)


def kernel(x, p, attn_norm, w_in, dn_conv_w, dn_a_log, dn_dt_bias, dn_norm, gla_w2, gla_b2, gla_norm, s5_lam_re, s5_lam_im, s5_log_step, s5_b_re, s5_b_im, s5_c_re, s5_c_im, s5_d, s5_w_glu, s5_b_glu, w_gate, b_gate, w_branch, w_o, ffn_norm, w_up, ffn_conv_w, ffn_conv_b, w_down, ple_norm, w_ple_gate, w_ple_proj, final_norm):
    bsz, seq, dm = x.shape
    depth = w_in.shape[0]
    t = bsz * seq
    tb = min(512, seq)
    tm = min(512, seq)
    x2 = x.reshape(t, dm)
    p3 = p.reshape(depth, t, p.shape[-1])
    row = lambda a: a.reshape(1, -1)
    for i in range(depth):
        w_in_r = _reorder_w_in(w_in[i]).astype(BF16)
        z2, sm2, ut = _inproj(x2, row(attn_norm[i]), w_in_r, tm=min(1024, t), tn=Z_WIDTH // 3)
        z3 = z2.reshape(bsz, seq, Z_WIDTH)
        sm3 = sm2.reshape(bsz, seq, LANES)
        ya = _deltanet(z3, sm3, dn_conv_w[i], _lane_row(dn_a_log[i], SM_A), _lane_row(dn_dt_bias[i], SM_A),
                       row(dn_norm[i]), tb)
        w2p = jnp.zeros((LANES, GLA_HEADS * GLA_DK), F32).at[SM_LR:SM_LR + GLA_RANK].set(gla_w2[i])
        yb = _gla(z3, sm3, w2p.astype(BF16), row(gla_b2[i]), row(gla_norm[i]), tb)
        s5_ops = _s5_operators(s5_lam_re[i], s5_lam_im[i], s5_log_step[i], s5_b_re[i],
                               s5_b_im[i], s5_c_re[i], s5_c_im[i])
        yt = _s5(ut.reshape(S5_WIDTH, t // S5_TC, S5_TC), *s5_ops, nbatch=bsz, ng=2)
        x2 = _merge(x2, ya.reshape(t, -1), yb.reshape(t, -1), yt.reshape(S5_WIDTH, t), z2, row(attn_norm[i]),
                    w_gate[i].astype(BF16), row(b_gate[i]), w_branch[i].astype(BF16), w_o[i].astype(BF16),
                    row(s5_d[i]), s5_w_glu[i].astype(BF16), row(s5_b_glu[i]), tm)
        x2 = _ffn(x2, p3, i, row(ffn_norm[i]), w_up[i].astype(BF16),
                  ffn_conv_w[i], row(ffn_conv_b[i]), w_down[i].astype(BF16),
                  row(ple_norm[i]), w_ple_gate[i].astype(BF16), w_ple_proj[i].astype(BF16),
                  row(final_norm), seq, tm, 256, i == depth - 1)
    return x2.reshape(bsz, seq, dm)
```

```python
import functools
import math

import jax
import jax.numpy as jnp
from jax import lax
from jax.experimental import pallas as pl
from jax.experimental.pallas import tpu as pltpu

F32 = jnp.float32
BF16 = jnp.bfloat16

EPS = 1e-6
CHUNK = 64
DN_HEADS, DN_DK, DN_DV, DN_CONV = 4, 128, 128, 4
GLA_HEADS, GLA_DK, GLA_DV, GLA_RANK = 4, 64, 128, 16
GLA_NORMALIZER = 16.0
S5_WIDTH, S5_GROUP, S5_STATE = 512, 16, 64
S5_GROUPS = S5_WIDTH // S5_GROUP
S5_TC = 128
FFN_CONV = 3
LANES = 128
HALO = 8

Z_QKV, Z_GATE, Z_GQK, Z_GV, Z_GR, Z_S5U, Z_SMALL = 0, 1536, 2048, 2560, 3072, 3584, 4096
Z_WIDTH = 4224
SM_B, SM_A, SM_LR = 0, 4, 8

VMEM_LIMIT = 56 * 1024 * 1024


def _cparams(sem):
    return pltpu.CompilerParams(dimension_semantics=sem, vmem_limit_bytes=VMEM_LIMIT)


def _mm(a, b):
    return jnp.dot(a.astype(BF16), b.astype(BF16), preferred_element_type=F32)


def _mm_nt(a, b):
    nb = a.ndim - 2
    batch = tuple(range(nb))
    return lax.dot_general(a.astype(BF16), b.astype(BF16),
                           (((a.ndim - 1,), (b.ndim - 1,)), (batch, batch)),
                           preferred_element_type=F32)


def _mm_nn(a, b):
    nb = a.ndim - 2
    batch = tuple(range(nb))
    return lax.dot_general(a.astype(BF16), b.astype(BF16),
                           (((a.ndim - 1,), (b.ndim - 2,)), (batch, batch)),
                           preferred_element_type=F32)


def _mm_tn(a, b):
    nb = a.ndim - 2
    batch = tuple(range(nb))
    return lax.dot_general(a.astype(BF16), b.astype(BF16),
                           (((a.ndim - 2,), (b.ndim - 2,)), (batch, batch)),
                           preferred_element_type=F32)


def _split3(a):
    a1 = a.astype(BF16)
    r1 = a - a1.astype(F32)
    a2 = r1.astype(BF16)
    a3 = (r1 - a2.astype(F32)).astype(BF16)
    return a1, a2, a3


def _cumsum_rows(lmat, a):
    a1, a2, a3 = _split3(a)
    d = lambda p: jnp.dot(lmat, p, preferred_element_type=F32)
    return d(a1) + d(a2) + d(a3)


def _mm_nn3(a, b):
    a1 = a.astype(BF16)
    a2 = (a - a1.astype(F32)).astype(BF16)
    b1 = b.astype(BF16)
    b2 = (b - b1.astype(F32)).astype(BF16)
    return _mm_nn(a1, b1) + _mm_nn(a1, b2) + _mm_nn(a2, b1)


def _rms(xf, g):
    ms = jnp.mean(xf * xf, axis=-1, keepdims=True)
    return xf * lax.rsqrt(ms + EPS) * g


def _sigmoid(x):
    return 0.5 * jnp.tanh(0.5 * x) + 0.5


def _softplus(x):
    return jnp.maximum(x, 0.0) + jnp.log(1.0 + jnp.exp(-jnp.abs(x)))


def _gelu_tanh(x):
    c = math.sqrt(2.0 / math.pi)
    return 0.5 * x * (1.0 + jnp.tanh(c * (x + 0.044715 * (x * x * x))))


def _chunk_tril(tb):
    r = lax.broadcasted_iota(jnp.int32, (tb, tb), 0)
    c = lax.broadcasted_iota(jnp.int32, (tb, tb), 1)
    same = (r // CHUNK) == (c // CHUNK)
    return jnp.where(same & (c <= r), 1.0, 0.0).astype(BF16)


def _inproj_kernel(x_ref, g_ref, w_ref, o_ref, sm_ref, ut_ref, *, chunk):
    h = _rms(x_ref[...], g_ref[...]).astype(BF16)
    n = w_ref.shape[1]
    tail = jnp.dot(h, w_ref[:, Z_S5U:n], preferred_element_type=F32)
    o_ref[:, Z_S5U:n] = tail.astype(o_ref.dtype)
    sm_ref[...] = tail[:, Z_SMALL - Z_S5U:Z_SMALL - Z_S5U + LANES]
    ut_ref[...] = tail[:, 0:S5_WIDTH].T.astype(BF16)
    for c0 in range(0, Z_S5U, chunk):
        o_ref[:, c0:c0 + chunk] = jnp.dot(h, w_ref[:, c0:c0 + chunk],
                                          preferred_element_type=F32).astype(o_ref.dtype)


def _inproj(x2, g, w, tm, chunk):
    t, d = x2.shape
    n = w.shape[1]
    assert n == Z_WIDTH and Z_S5U % chunk == 0
    return pl.pallas_call(
        functools.partial(_inproj_kernel, chunk=chunk),
        grid=(t // tm,),
        in_specs=[pl.BlockSpec((tm, d), lambda i: (i, 0)),
                  pl.BlockSpec((1, d), lambda i: (0, 0)),
                  pl.BlockSpec((d, n), lambda i: (0, 0), pipeline_mode=pl.Buffered(1))],
        out_specs=[pl.BlockSpec((tm, n), lambda i: (i, 0)),
                   pl.BlockSpec((tm, LANES), lambda i: (i, 0)),
                   pl.BlockSpec((S5_WIDTH, tm), lambda i: (0, i))],
        out_shape=[jax.ShapeDtypeStruct((t, n), BF16),
                   jax.ShapeDtypeStruct((t, LANES), F32),
                   jax.ShapeDtypeStruct((S5_WIDTH, t), BF16)],
        compiler_params=_cparams(("parallel",)),
        name="inproj",
    )(x2, g, w)


def _deltanet_body(qkv_ref, sm_ref, gate_ref, cw_ref, alog_ref, dtb_ref, ng_ref, o_ref,
                   ext_ref, s_ref, *, tb):
    nh, dk, dv = DN_HEADS, DN_DK, DN_DV
    qk_w = nh * dk

    ext_ref[HALO:HALO + tb, :] = qkv_ref[0].astype(F32)
    ext = ext_ref[...]
    acc = cw_ref[0:1, :] * ext
    for kk in range(1, DN_CONV):
        acc = pltpu.roll(acc, 1, 0) + cw_ref[kk:kk + 1, :] * ext
    ext_ref[0:HALO, :] = ext[tb:tb + HALO, :]
    acc = acc[HALO:HALO + tb, :]
    qkv = acc * _sigmoid(acc)

    sm = sm_ref[0]
    lane = lax.broadcasted_iota(jnp.int32, (1, LANES), 1)
    coef = jnp.where((lane >= SM_A) & (lane < SM_A + nh), -jnp.exp(alog_ref[...]), 0.0)
    beta_all = _sigmoid(sm)
    glog = coef * _softplus(sm + dtb_ref[...])
    ltri = _chunk_tril(tb)
    gam_col = _cumsum_rows(ltri, glog)
    gam_row = gam_col.T

    ri = lax.broadcasted_iota(jnp.int32, (CHUNK, CHUNK), 0)
    ci = lax.broadcasted_iota(jnp.int32, (CHUNK, CHUNK), 1)
    incl = (ci <= ri)[None]
    strict = (ci < ri)[None]
    eye = jnp.where(ci == ri, 1.0, 0.0)[None]

    nc = tb // CHUNK
    pairs = [(c, h) for c in range(nc) for h in range(nh)]
    rows = lambda c: slice(c * CHUNK, (c + 1) * CHUNK)
    q = jnp.stack([qkv[rows(c), h * dk:(h + 1) * dk] for c, h in pairs])
    k = jnp.stack([qkv[rows(c), qk_w + h * dk:qk_w + (h + 1) * dk] for c, h in pairs])
    v = jnp.stack([qkv[rows(c), 2 * qk_w + h * dv:2 * qk_w + (h + 1) * dv] for c, h in pairs])
    q = q * lax.rsqrt(jnp.sum(q * q, axis=-1, keepdims=True) + EPS) * (dk ** -0.5)
    k = k * lax.rsqrt(jnp.sum(k * k, axis=-1, keepdims=True) + EPS)
    gcol = jnp.stack([gam_col[rows(c), SM_A + h:SM_A + h + 1] for c, h in pairs])
    grow = jnp.stack([gam_row[SM_A + h:SM_A + h + 1, rows(c)] for c, h in pairs])
    bcol = jnp.stack([beta_all[rows(c), SM_B + h:SM_B + h + 1] for c, h in pairs])

    dec = jnp.exp(jnp.where(incl, gcol - grow, -jnp.inf))
    kkm = _mm_nt(k, k)
    a_pos = bcol * kkm * jnp.where(strict, dec, 0.0)
    tinv = eye - a_pos
    mp = -a_pos
    for _ in range(5):
        mp = _mm_nn(mp, mp)
        tinv = tinv + _mm_nn(tinv, mp)
    eg = jnp.exp(gcol)
    rhs = jnp.concatenate([bcol * v, (bcol * eg) * k], axis=-1)
    sol = _mm_nn(tinv, rhs)
    resid = rhs - sol - _mm_nn3(a_pos, sol)
    sol = sol + _mm_nn(tinv, resid)
    u_new, w = sol[..., :dv], sol[..., dv:]
    attn = _mm_nt(q, k) * dec
    q_dec = q * eg
    gend = gcol[:, CHUNK - 1:CHUNK, :]
    k_dec = k * jnp.exp(gend - gcol)
    g_end = jnp.exp(gend)

    s_cur = s_ref[...]
    for c in range(nc):
        hs = slice(c * nh, (c + 1) * nh)
        u = u_new[hs] - _mm_nn(w[hs], s_cur)
        o = _mm_nn(q_dec[hs], s_cur) + _mm_nn(attn[hs], u)
        s_cur = g_end[hs] * s_cur + _mm_tn(k_dec[hs], u)
        o = _rms(o, ng_ref[...][None])
        for h in range(nh):
            g = gate_ref[0, rows(c), h * dv:(h + 1) * dv].astype(F32)
            o_ref[0, rows(c), h * dv:(h + 1) * dv] = (o[h] * (g * _sigmoid(g))).astype(o_ref.dtype)
    s_ref[...] = s_cur


def _mixers_kernel(qkv_ref, sm_ref, gate_ref, cw_ref, alog_ref, dtb_ref, dng_ref,
                   gqk_ref, gv_ref, gr_ref, w2_ref, b2_ref, gng_ref, oa_ref, ob_ref,
                   ext_ref, s_ref, st_ref, *, tb):
    @pl.when(pl.program_id(1) == 0)
    def _():
        ext_ref[0:HALO, :] = jnp.zeros((HALO, ext_ref.shape[1]), F32)
        s_ref[...] = jnp.zeros(s_ref.shape, F32)
        st_ref[...] = jnp.zeros(st_ref.shape, F32)

    _deltanet_body(qkv_ref, sm_ref, gate_ref, cw_ref, alog_ref, dtb_ref, dng_ref, oa_ref, ext_ref, s_ref, tb=tb)
    _gla_body(gqk_ref, gv_ref, gr_ref, sm_ref, w2_ref, b2_ref, gng_ref, ob_ref, st_ref, tb=tb)


def _mixers(z3, sm3, conv_w, alog_row, dtb_row, dn_norm_g, w2p, b2, gla_norm_g, tb):
    b, l, _ = z3.shape
    w_qkv = 3 * DN_HEADS * DN_DK
    const = lambda bi, j: (0, 0)
    zblk = lambda off: pl.BlockSpec((1, tb, 512), lambda bi, j: (bi, j, off // 512))
    out_blk = pl.BlockSpec((1, tb, 512), lambda bi, j: (bi, j, 0))
    return pl.pallas_call(
        functools.partial(_mixers_kernel, tb=tb),
        grid=(b, l // tb),
        in_specs=[pl.BlockSpec((1, tb, w_qkv), lambda bi, j: (bi, j, Z_QKV // w_qkv)),
                  pl.BlockSpec((1, tb, LANES), lambda bi, j: (bi, j, 0)),
                  zblk(Z_GATE),
                  pl.BlockSpec((DN_CONV, w_qkv), const),
                  pl.BlockSpec((1, LANES), const),
                  pl.BlockSpec((1, LANES), const),
                  pl.BlockSpec((1, DN_DV), const),
                  zblk(Z_GQK), zblk(Z_GV), zblk(Z_GR),
                  pl.BlockSpec((LANES, GLA_HEADS * GLA_DK), const),
                  pl.BlockSpec((1, GLA_HEADS * GLA_DK), const),
                  pl.BlockSpec((1, GLA_DV), const)],
        out_specs=[out_blk, out_blk],
        out_shape=[jax.ShapeDtypeStruct((b, l, DN_HEADS * DN_DV), BF16),
                   jax.ShapeDtypeStruct((b, l, GLA_HEADS * GLA_DV), BF16)],
        scratch_shapes=[pltpu.VMEM((tb + HALO, w_qkv), F32),
                        pltpu.VMEM((DN_HEADS, DN_DK, DN_DV), F32),
                        pltpu.VMEM((GLA_HEADS, GLA_DV, GLA_DK), F32)],
        compiler_params=_cparams(("parallel", "arbitrary")),
        name="mixers",
    )(z3, sm3, z3, conv_w, alog_row, dtb_row, dn_norm_g, z3, z3, z3, w2p, b2, gla_norm_g)


def _gla_body(qk_ref, v_ref, r_ref, sm_ref, w2_ref, b2_ref, ng_ref, o_ref, st_ref, *, tb):
    nh, dk, dv = GLA_HEADS, GLA_DK, GLA_DV
    qk_w = nh * dk

    z = _mm(sm_ref[0], w2_ref[...]) + b2_ref[...]
    glog = -_softplus(-z) * (1.0 / GLA_NORMALIZER)
    bcum = _cumsum_rows(_chunk_tril(tb), glog)
    qk = qk_ref[0].astype(F32)
    q = qk[:, :qk_w] * (dk ** -0.5)
    k = qk[:, qk_w:]
    q_e = q * jnp.exp(bcum)
    k_e = k * jnp.exp(-bcum)

    ri = lax.broadcasted_iota(jnp.int32, (CHUNK, CHUNK), 0)
    ci = lax.broadcasted_iota(jnp.int32, (CHUNK, CHUNK), 1)
    incl = ci <= ri

    nc = tb // CHUNK
    pairs = [(c, h) for c in range(nc) for h in range(nh)]
    rows = lambda c: slice(c * CHUNK, (c + 1) * CHUNK)
    hsl = lambda h: slice(h * dk, (h + 1) * dk)
    b_end = [bcum[(c + 1) * CHUNK - 1:(c + 1) * CHUNK, :] for c in range(nc)]
    k_dec = [k[rows(c)] * jnp.exp(b_end[c] - bcum[rows(c)]) for c in range(nc)]
    qe_p = jnp.stack([q_e[rows(c), hsl(h)] for c, h in pairs])
    ke_p = jnp.stack([k_e[rows(c), hsl(h)] for c, h in pairs])
    kd_p = jnp.stack([k_dec[c][:, hsl(h)] for c, h in pairs])
    v_p = jnp.stack([v_ref[0, rows(c), h * dv:(h + 1) * dv] for c, h in pairs])
    ge_p = jnp.stack([jnp.exp(b_end[c][:, hsl(h)]) for c, h in pairs])
    attn = jnp.where(incl[None], _mm_nt(qe_p, ke_p), 0.0)
    intra = _mm_nn(attn, v_p)
    dst = _mm_tn(v_p, kd_p)

    st = st_ref[...]
    for c in range(nc):
        ps = slice(c * nh, (c + 1) * nh)
        o = intra[ps] + _mm_nt(qe_p[ps], st)
        st = ge_p[ps] * st + dst[ps]
        o = _rms(o, ng_ref[...][None])
        for h in range(nh):
            vs = slice(h * dv, (h + 1) * dv)
            g = r_ref[0, rows(c), vs].astype(F32)
            o_ref[0, rows(c), vs] = (o[h] * (g * _sigmoid(g))).astype(o_ref.dtype)
    st_ref[...] = st


def _s5_kernel(u_ref, p_ref, pw_ref, q_ref, r_ref, bb_ref, c_ref, ap_ref, y_ref,
               m_ref, taps_ref, ef_ref, es_ref, hp_ref, *, nbatch, ng):
    tc = S5_TC
    nrow = u_ref.shape[1]
    nchunk = nrow // nbatch

    def chunk_rows(g):
        return jnp.concatenate([u_ref[g * S5_GROUP + j] for j in range(S5_GROUP)], axis=-1)

    for g in range(ng):
        qr, qi = q_ref[g, :, 0:LANES], q_ref[g, :, LANES:2 * LANES]
        for j in range(S5_GROUP):
            br, bi = bb_ref[g, j:j + 1, 0:LANES], bb_ref[g, j:j + 1, LANES:2 * LANES]
            ef_ref[j * tc:(j + 1) * tc, 0:LANES] = (qr * br - qi * bi).astype(BF16)
            ef_ref[j * tc:(j + 1) * tc, LANES:2 * LANES] = (qr * bi + qi * br).astype(BF16)
        es_ref[g] = jnp.dot(chunk_rows(g), ef_ref[...], preferred_element_type=F32)

    ars = [ap_ref[g, 0:1, :] for g in range(ng)]
    ais = [ap_ref[g, 1:2, :] for g in range(ng)]

    def body(step, carry):
        new = []
        for g in range(ng):
            for b in range(nbatch):
                hr, hi = carry[2 * (g * nbatch + b)], carry[2 * (g * nbatch + b) + 1]
                base = pl.multiple_of(b * nchunk + step * HALO, HALO)
                et = es_ref[g, pl.ds(base, HALO), :]
                hrs, his = [], []
                for i in range(HALO):
                    hrs.append(hr)
                    his.append(hi)
                    er, ei = et[i:i + 1, 0:LANES], et[i:i + 1, LANES:2 * LANES]
                    hr, hi = ars[g] * hr - ais[g] * hi + er, ars[g] * hi + ais[g] * hr + ei
                hp_ref[g, pl.ds(base, HALO), 0:LANES] = jnp.concatenate(hrs, axis=0)
                hp_ref[g, pl.ds(base, HALO), LANES:2 * LANES] = jnp.concatenate(his, axis=0)
                new += [hr, hi]
        return tuple(new)

    zero = jnp.zeros((1, LANES), F32)
    lax.fori_loop(0, nchunk // HALO, body, (zero,) * (2 * nbatch * ng))

    causal = (lax.broadcasted_iota(jnp.int32, (tc, tc), 1) >= lax.broadcasted_iota(jnp.int32, (tc, tc), 0))
    for g in range(ng):
        taps_ref[...] = _mm_nn3(p_ref[g], pw_ref[g])

        def build(j, carry):
            kt = taps_ref[pl.ds(pl.multiple_of(j * S5_GROUP, S5_GROUP), S5_GROUP), :]
            for i in range(S5_GROUP):
                blk = pltpu.roll(jnp.broadcast_to(kt[i:i + 1, :], (tc, tc)), 0, 1, stride=1, stride_axis=0)
                m_ref[pl.ds(pl.multiple_of(j * tc, tc), tc), i * tc:(i + 1) * tc] = (
                    jnp.where(causal, blk, 0.0).astype(BF16))
            return carry

        lax.fori_loop(0, S5_GROUP, build, 0)
        y = jnp.dot(chunk_rows(g), m_ref[...], preferred_element_type=F32)
        rr, ri = r_ref[g, :, 0:LANES], r_ref[g, :, LANES:2 * LANES]
        for i in range(S5_GROUP):
            cr, ci = c_ref[g, i:i + 1, 0:LANES], c_ref[g, i:i + 1, LANES:2 * LANES]
            ef_ref[i * tc:(i + 1) * tc, 0:LANES] = (rr * cr - ri * ci).astype(BF16)
            ef_ref[i * tc:(i + 1) * tc, LANES:2 * LANES] = (-(rr * ci + ri * cr)).astype(BF16)
        y = y + _mm_nt(hp_ref[g], ef_ref[...])
        for i in range(S5_GROUP):
            y_ref[g * S5_GROUP + i] = y[:, i * tc:(i + 1) * tc]


def _s5(ut3, p_op, pw_op, q_op, r_op, bb_op, c_op, ap, nbatch, ng):
    width, r, tc = ut3.shape
    g = width // S5_GROUP
    kw = S5_GROUP * tc
    kern = functools.partial(_s5_kernel, nbatch=nbatch, ng=ng)
    blk = lambda *shape: pl.BlockSpec((ng,) + shape, lambda i: (i, 0, 0))
    chan = pl.BlockSpec((ng * S5_GROUP, r, tc), lambda i: (i, 0, 0))
    return pl.pallas_call(
        kern,
        grid=(g // ng,),
        in_specs=[chan, blk(S5_GROUP * S5_GROUP, 2 * S5_STATE), blk(2 * S5_STATE, tc),
                  blk(tc, 2 * LANES), blk(tc, 2 * LANES), blk(S5_GROUP, 2 * LANES), blk(S5_GROUP, 2 * LANES),
                  blk(HALO, LANES)],
        out_specs=chan,
        out_shape=jax.ShapeDtypeStruct((width, r, tc), F32),
        scratch_shapes=[pltpu.VMEM((kw, kw), BF16),
                        pltpu.VMEM((S5_GROUP * S5_GROUP, tc), F32),
                        pltpu.VMEM((kw, 2 * LANES), BF16),
                        pltpu.VMEM((ng, r, 2 * LANES), F32),
                        pltpu.VMEM((ng, r, 2 * LANES), F32)],
        compiler_params=_cparams(("parallel",)),
        name="s5",
    )(ut3, p_op, pw_op, q_op, r_op, bb_op, c_op, ap)


def _s5_operators(lam_re, lam_im, log_step, b_re, b_im, c_re, c_im):
    tc, n, gs = S5_TC, S5_STATE, S5_GROUPS
    step = jnp.exp(log_step)[:, None]
    lr, li = lam_re, lam_im
    mag = jnp.exp(lr * step)
    abar_re, abar_im = mag * jnp.cos(li * step), mag * jnp.sin(li * step)
    den = lr * lr + li * li
    nr, ni = abar_re - 1.0, abar_im
    fr = (nr * lr + ni * li) / den
    fi = (ni * lr - nr * li) / den
    bb_re = fr[..., None] * b_re - fi[..., None] * b_im
    bb_im = fr[..., None] * b_im + fi[..., None] * b_re
    tau = jnp.arange(tc + 1, dtype=F32)[None, None, :]
    pmag = jnp.exp((lr * step)[..., None] * tau)
    ang = (li * step)[..., None] * tau
    p_re, p_im = pmag * jnp.cos(ang), pmag * jnp.sin(ang)
    bbt_re, bbt_im = bb_re.transpose(0, 2, 1)[:, :, None, :], bb_im.transpose(0, 2, 1)[:, :, None, :]
    cb_re = c_re[:, None] * bbt_re - c_im[:, None] * bbt_im
    cb_im = c_re[:, None] * bbt_im + c_im[:, None] * bbt_re
    p_op = jnp.concatenate([cb_re, -cb_im], axis=-1).reshape(gs, S5_GROUP * S5_GROUP, 2 * n)
    pw_op = jnp.concatenate([p_re[..., :tc], p_im[..., :tc]], axis=1)
    def lanes4(re, im):
        pad = jnp.zeros(re.shape[:-1] + (LANES - n,), F32)
        return jnp.concatenate([re, pad, im, pad], axis=-1)

    q_op = lanes4(p_re[..., :tc][..., ::-1].transpose(0, 2, 1), p_im[..., :tc][..., ::-1].transpose(0, 2, 1))
    r_op = lanes4(p_re[..., 1:].transpose(0, 2, 1), p_im[..., 1:].transpose(0, 2, 1))
    bb_op = lanes4(bb_re.transpose(0, 2, 1), bb_im.transpose(0, 2, 1))
    c_op = lanes4(c_re, c_im)
    ap = jnp.zeros((gs, HALO, LANES), F32)
    ap = ap.at[:, 0, :n].set(p_re[..., tc]).at[:, 1, :n].set(p_im[..., tc])
    return p_op, pw_op, q_op, r_op, bb_op, c_op, ap


def _merge_kernel(x_ref, ya_ref, yb_ref, ys_ref, u_ref, ng_ref, wg_ref, bg_ref, wb_ref, wo_ref,
                  d_ref, wglu_ref, bglu_ref, o_ref):
    x = x_ref[...]
    d = x.shape[1]
    h = _rms(x, ng_ref[...]).astype(BF16)
    yc = _gelu_tanh(ys_ref[...].T + d_ref[...] * u_ref[...].astype(F32))
    yc = yc * _sigmoid(_mm(yc, wglu_ref[...]) + bglu_ref[...])
    merged = None
    for i, y in enumerate((ya_ref[...], yb_ref[...], yc.astype(BF16))):
        gate = _sigmoid(jnp.dot(h, wg_ref[:, i * d:(i + 1) * d], preferred_element_type=F32)
                        + bg_ref[:, i * d:(i + 1) * d])
        term = gate * jnp.dot(y, wb_ref[i], preferred_element_type=F32)
        merged = term if merged is None else merged + term
    o_ref[...] = x + _mm(merged, wo_ref[...])


def _merge(x2, ya, yb, ys, z2, ng, wg, bg, wb, wo, d, wglu, bglu, tm):
    t, dm = x2.shape
    bw = ya.shape[1]
    row = lambda w: pl.BlockSpec((tm, w), lambda i: (i, 0))
    const2 = lambda a: pl.BlockSpec(a.shape, lambda i: (0,) * a.ndim)
    return pl.pallas_call(
        _merge_kernel,
        grid=(t // tm,),
        in_specs=[row(dm), row(bw), row(bw),
                  pl.BlockSpec((S5_WIDTH, tm), lambda i: (0, i)),
                  pl.BlockSpec((tm, S5_WIDTH), lambda i: (i, Z_S5U // S5_WIDTH)),
                  const2(ng), const2(wg), const2(bg), const2(wb), const2(wo),
                  const2(d), const2(wglu), const2(bglu)],
        out_specs=row(dm),
        out_shape=jax.ShapeDtypeStruct((t, dm), F32),
        compiler_params=_cparams(("parallel",)),
        name="merge",
    )(x2, ya, yb, ys, z2, ng, wg, bg, wb, wo, d, wglu, bglu)


def _ffn_kernel(x_ref, p_ref, ng_ref, wu_ref, cw_ref, cb_ref, wd_ref, png_ref, wpg_ref,
                wpp_ref, fg_ref, o_ref, tail_ref, act_ref, *, tm, tiles_per_seq, ffc, down_group, final):
    dff = wd_ref.shape[0]

    @pl.when(pl.program_id(0) % tiles_per_seq == 0)
    def _():
        tail_ref[...] = jnp.zeros(tail_ref.shape, F32)

    x = x_ref[...]
    h = _rms(x, ng_ref[...]).astype(BF16)

    def up(c0):
        return (jnp.dot(h, wu_ref[:, c0:c0 + ffc], preferred_element_type=F32),
                jnp.dot(h, wu_ref[:, dff + c0:dff + c0 + ffc], preferred_element_type=F32))

    starts = list(range(0, dff, ffc))
    acc = None
    group0 = 0
    gu = up(0)
    for idx, c0 in enumerate(starts):
        cols = slice(c0, c0 + ffc)
        g, u = gu
        if idx + 1 < len(starts):
            gu = up(starts[idx + 1])
        ext = jnp.concatenate([tail_ref[:, cols], g], axis=0)
        tail_ref[:, cols] = g[tm - HALO:tm, :]
        cv = cw_ref[0:1, cols] * ext
        for kk in range(1, FFN_CONV):
            cv = pltpu.roll(cv, 1, 0) + cw_ref[kk:kk + 1, cols] * ext
        act_ref[:, cols] = (_gelu_tanh(cv[HALO:HALO + tm, :] + cb_ref[:, cols]) * u).astype(BF16)
        if (idx + 1) % down_group == 0 or idx + 1 == len(starts):
            grp = slice(group0, c0 + ffc)
            part = jnp.dot(act_ref[:, grp], wd_ref[grp, :], preferred_element_type=F32)
            acc = part if acc is None else acc + part
            group0 = c0 + ffc

    xn = x + acc
    gate = _sigmoid(_mm(_rms(xn, png_ref[...]), wpg_ref[...]))
    out = xn + gate * _mm(p_ref[0], wpp_ref[...])
    if final:
        out = _rms(out, fg_ref[...])
    o_ref[...] = out


def _ffn(x2, p3, layer, ng, wu, cw, cb, wd, png, wpg, wpp, fg, seq, tm, ffc, final):
    t, dm = x2.shape
    dff = wd.shape[0]
    assert dff % ffc == 0 and ffc % LANES == 0
    kern = functools.partial(_ffn_kernel, tm=tm, tiles_per_seq=seq // tm, ffc=ffc, down_group=4, final=final)
    const = lambda a: pl.BlockSpec(a.shape, lambda i: (0,) * a.ndim, pipeline_mode=pl.Buffered(1))
    return pl.pallas_call(
        kern,
        grid=(t // tm,),
        in_specs=[pl.BlockSpec((tm, dm), lambda i: (i, 0)),
                  pl.BlockSpec((1, tm, p3.shape[2]), lambda i: (layer, i, 0)),
                  const(ng), const(wu), const(cw), const(cb), const(wd),
                  const(png), const(wpg), const(wpp), const(fg)],
        out_specs=pl.BlockSpec((tm, dm), lambda i: (i, 0)),
        out_shape=jax.ShapeDtypeStruct((t, dm), F32),
        scratch_shapes=[pltpu.VMEM((HALO, dff), F32), pltpu.VMEM((tm, dff), BF16)],
        compiler_params=_cparams(("arbitrary",)),
        name="ffn",
    )(x2, p3, ng, wu, cw, cb, wd, png, wpg, wpp, fg)


def _reorder_w_in(w_in):
    qk, v = DN_HEADS * DN_DK, DN_HEADS * DN_DV
    gqk, gv = GLA_HEADS * GLA_DK, GLA_HEADS * GLA_DV
    o = 0
    seg = {}
    for name, width in (("dn_q", qk), ("dn_k", qk), ("dn_v", v), ("dn_b", DN_HEADS), ("dn_a", DN_HEADS),
                        ("dn_g", v), ("gl_q", gqk), ("gl_k", gqk), ("gl_v", gv), ("gl_lr", GLA_RANK),
                        ("gl_r", gv), ("s5_u", S5_WIDTH)):
        seg[name] = w_in[:, o:o + width]
        o += width
    d = w_in.shape[0]
    small = jnp.concatenate([seg["dn_b"], seg["dn_a"], seg["gl_lr"],
                             jnp.zeros((d, LANES - 2 * DN_HEADS - GLA_RANK), w_in.dtype)], axis=1)
    return jnp.concatenate([seg["dn_q"], seg["dn_k"], seg["dn_v"], seg["dn_g"], seg["gl_q"], seg["gl_k"],
                            seg["gl_v"], seg["gl_r"], seg["s5_u"], small], axis=1)


def _lane_row(vals, offset):
    return jnp.zeros((1, LANES), F32).at[0, offset:offset + vals.shape[0]].set(vals)


def kernel(x, p, attn_norm, w_in, dn_conv_w, dn_a_log, dn_dt_bias, dn_norm, gla_w2, gla_b2, gla_norm, s5_lam_re, s5_lam_im, s5_log_step, s5_b_re, s5_b_im, s5_c_re, s5_c_im, s5_d, s5_w_glu, s5_b_glu, w_gate, b_gate, w_branch, w_o, ffn_norm, w_up, ffn_conv_w, ffn_conv_b, w_down, ple_norm, w_ple_gate, w_ple_proj, final_norm):
    bsz, seq, dm = x.shape
    depth = w_in.shape[0]
    t = bsz * seq
    tb = min(512, seq)
    tm = min(512, seq)
    x2 = x.reshape(t, dm)
    p3 = p.reshape(depth, t, p.shape[-1])
    row = lambda a: a.reshape(1, -1)
    for i in range(depth):
        w_in_r = _reorder_w_in(w_in[i]).astype(BF16)
        z2, sm2, ut = _inproj(x2, row(attn_norm[i]), w_in_r, tm=tm, chunk=512)
        z3 = z2.reshape(bsz, seq, Z_WIDTH)
        sm3 = sm2.reshape(bsz, seq, LANES)
        w2p = jnp.zeros((LANES, GLA_HEADS * GLA_DK), F32).at[SM_LR:SM_LR + GLA_RANK].set(gla_w2[i])
        ya, yb = _mixers(z3, sm3, dn_conv_w[i], _lane_row(dn_a_log[i], SM_A), _lane_row(dn_dt_bias[i], SM_A),
                         row(dn_norm[i]), w2p.astype(BF16), row(gla_b2[i]), row(gla_norm[i]), tb)
        s5_ops = _s5_operators(s5_lam_re[i], s5_lam_im[i], s5_log_step[i], s5_b_re[i],
                               s5_b_im[i], s5_c_re[i], s5_c_im[i])
        yt = _s5(ut.reshape(S5_WIDTH, t // S5_TC, S5_TC), *s5_ops, nbatch=bsz, ng=2)
        x2 = _merge(x2, ya.reshape(t, -1), yb.reshape(t, -1), yt.reshape(S5_WIDTH, t), z2, row(attn_norm[i]),
                    w_gate[i].astype(BF16), row(b_gate[i]), w_branch[i].astype(BF16), w_o[i].astype(BF16),
                    row(s5_d[i]), s5_w_glu[i].astype(BF16), row(s5_b_glu[i]), tm)
        x2 = _ffn(x2, p3, i, row(ffn_norm[i]), w_up[i].astype(BF16),
                  ffn_conv_w[i], row(ffn_conv_b[i]), w_down[i].astype(BF16),
                  row(ple_norm[i]), w_ple_gate[i].astype(BF16), w_ple_proj[i].astype(BF16),
                  row(final_norm), seq, tm, 256, i == depth - 1)
    return x2.reshape(bsz, seq, dm)
```

```python
import functools
import math

import jax
import jax.numpy as jnp
from jax import lax
from jax.experimental import pallas as pl
from jax.experimental.pallas import tpu as pltpu

F32 = jnp.float32
BF16 = jnp.bfloat16

EPS = 1e-6
CHUNK = 64
DN_HEADS, DN_DK, DN_DV, DN_CONV = 4, 128, 128, 4
GLA_HEADS, GLA_DK, GLA_DV, GLA_RANK = 4, 64, 128, 16
GLA_NORMALIZER = 16.0
S5_WIDTH, S5_GROUP, S5_STATE = 512, 16, 64
S5_GROUPS = S5_WIDTH // S5_GROUP
S5_TC = 128
FFN_CONV = 3
LANES = 128
HALO = 8

Z_QKV, Z_GATE, Z_GQK, Z_GV, Z_GR, Z_S5U, Z_SMALL = 0, 1536, 2048, 2560, 3072, 3584, 4096
Z_WIDTH = 4224
SM_B, SM_A, SM_LR = 0, 4, 8

VMEM_LIMIT = 56 * 1024 * 1024


def _cparams(sem):
    return pltpu.CompilerParams(dimension_semantics=sem, vmem_limit_bytes=VMEM_LIMIT)


def _layer_spec(a, layer):
    zeros = (0,) * (a.ndim - 1)
    return pl.BlockSpec((None,) + a.shape[1:], lambda *_: (layer,) + zeros, pipeline_mode=pl.Buffered(1))


def _mm(a, b):
    return jnp.dot(a.astype(BF16), b.astype(BF16), preferred_element_type=F32)


def _mm_nt(a, b):
    nb = a.ndim - 2
    batch = tuple(range(nb))
    return lax.dot_general(a.astype(BF16), b.astype(BF16),
                           (((a.ndim - 1,), (b.ndim - 1,)), (batch, batch)),
                           preferred_element_type=F32)


def _mm_nn(a, b):
    nb = a.ndim - 2
    batch = tuple(range(nb))
    return lax.dot_general(a.astype(BF16), b.astype(BF16),
                           (((a.ndim - 1,), (b.ndim - 2,)), (batch, batch)),
                           preferred_element_type=F32)


def _mm_tn(a, b):
    nb = a.ndim - 2
    batch = tuple(range(nb))
    return lax.dot_general(a.astype(BF16), b.astype(BF16),
                           (((a.ndim - 2,), (b.ndim - 2,)), (batch, batch)),
                           preferred_element_type=F32)


def _split3(a):
    a1 = a.astype(BF16)
    r1 = a - a1.astype(F32)
    a2 = r1.astype(BF16)
    a3 = (r1 - a2.astype(F32)).astype(BF16)
    return a1, a2, a3


def _cumsum_rows(lmat, a):
    a1, a2, a3 = _split3(a)
    d = lambda p: jnp.dot(lmat, p, preferred_element_type=F32)
    return d(a1) + d(a2) + d(a3)


def _mm_nn3(a, b):
    a1 = a.astype(BF16)
    a2 = (a - a1.astype(F32)).astype(BF16)
    b1 = b.astype(BF16)
    b2 = (b - b1.astype(F32)).astype(BF16)
    return _mm_nn(a1, b1) + _mm_nn(a1, b2) + _mm_nn(a2, b1)


def _rms(xf, g):
    ms = jnp.mean(xf * xf, axis=-1, keepdims=True)
    return xf * lax.rsqrt(ms + EPS) * g


def _sigmoid(x):
    return 0.5 * jnp.tanh(0.5 * x) + 0.5


def _softplus(x):
    return jnp.maximum(x, 0.0) + jnp.log(1.0 + jnp.exp(-jnp.abs(x)))


def _gelu_tanh(x):
    c = math.sqrt(2.0 / math.pi)
    return 0.5 * x * (1.0 + jnp.tanh(c * (x + 0.044715 * (x * x * x))))


def _chunk_tril(tb):
    r = lax.broadcasted_iota(jnp.int32, (tb, tb), 0)
    c = lax.broadcasted_iota(jnp.int32, (tb, tb), 1)
    same = (r // CHUNK) == (c // CHUNK)
    return jnp.where(same & (c <= r), 1.0, 0.0).astype(BF16)


def _inproj_kernel(x_ref, g_ref, w_ref, o_ref, sm_ref, ut_ref, *, chunk):
    h = _rms(x_ref[...], g_ref[...]).astype(BF16)
    n = w_ref.shape[1]
    tail = jnp.dot(h, w_ref[:, Z_S5U:n], preferred_element_type=F32)
    o_ref[:, Z_S5U:n] = tail.astype(o_ref.dtype)
    sm_ref[...] = tail[:, Z_SMALL - Z_S5U:Z_SMALL - Z_S5U + LANES]
    ut_ref[...] = tail[:, 0:S5_WIDTH].T.astype(BF16)
    for c0 in range(0, Z_S5U, chunk):
        o_ref[:, c0:c0 + chunk] = jnp.dot(h, w_ref[:, c0:c0 + chunk],
                                          preferred_element_type=F32).astype(o_ref.dtype)


def _inproj(x2, g, w, layer, tm, chunk):
    t, d = x2.shape
    n = w.shape[2]
    assert n == Z_WIDTH and Z_S5U % chunk == 0
    return pl.pallas_call(
        functools.partial(_inproj_kernel, chunk=chunk),
        grid=(t // tm,),
        in_specs=[pl.BlockSpec((tm, d), lambda i: (i, 0)),
                  pl.BlockSpec((1, d), lambda i: (0, 0)),
                  _layer_spec(w, layer)],
        out_specs=[pl.BlockSpec((tm, n), lambda i: (i, 0)),
                   pl.BlockSpec((tm, LANES), lambda i: (i, 0)),
                   pl.BlockSpec((S5_WIDTH, tm), lambda i: (0, i))],
        out_shape=[jax.ShapeDtypeStruct((t, n), BF16),
                   jax.ShapeDtypeStruct((t, LANES), F32),
                   jax.ShapeDtypeStruct((S5_WIDTH, t), BF16)],
        compiler_params=_cparams(("parallel",)),
        name="inproj",
    )(x2, g, w)


def _deltanet_body(qkv_ref, sm_ref, gate_ref, cw_ref, alog_ref, dtb_ref, ng_ref, o_ref,
                   ext_ref, s_ref, *, tb):
    nh, dk, dv = DN_HEADS, DN_DK, DN_DV
    qk_w = nh * dk

    ext_ref[HALO:HALO + tb, :] = qkv_ref[0].astype(F32)
    ext = ext_ref[...]
    acc = cw_ref[0:1, :] * ext
    for kk in range(1, DN_CONV):
        acc = pltpu.roll(acc, 1, 0) + cw_ref[kk:kk + 1, :] * ext
    ext_ref[0:HALO, :] = ext[tb:tb + HALO, :]
    acc = acc[HALO:HALO + tb, :]
    qkv = acc * _sigmoid(acc)

    sm = sm_ref[0]
    lane = lax.broadcasted_iota(jnp.int32, (1, LANES), 1)
    coef = jnp.where((lane >= SM_A) & (lane < SM_A + nh), -jnp.exp(alog_ref[...]), 0.0)
    beta_all = _sigmoid(sm)
    glog = coef * _softplus(sm + dtb_ref[...])
    ltri = _chunk_tril(tb)
    gam_col = _cumsum_rows(ltri, glog)
    gam_row = gam_col.T

    ri = lax.broadcasted_iota(jnp.int32, (CHUNK, CHUNK), 0)
    ci = lax.broadcasted_iota(jnp.int32, (CHUNK, CHUNK), 1)
    incl = (ci <= ri)[None]
    strict = (ci < ri)[None]
    eye = jnp.where(ci == ri, 1.0, 0.0)[None]

    nc = tb // CHUNK
    pairs = [(c, h) for c in range(nc) for h in range(nh)]
    rows = lambda c: slice(c * CHUNK, (c + 1) * CHUNK)
    q = jnp.stack([qkv[rows(c), h * dk:(h + 1) * dk] for c, h in pairs])
    k = jnp.stack([qkv[rows(c), qk_w + h * dk:qk_w + (h + 1) * dk] for c, h in pairs])
    v = jnp.stack([qkv[rows(c), 2 * qk_w + h * dv:2 * qk_w + (h + 1) * dv] for c, h in pairs])
    q = q * lax.rsqrt(jnp.sum(q * q, axis=-1, keepdims=True) + EPS) * (dk ** -0.5)
    k = k * lax.rsqrt(jnp.sum(k * k, axis=-1, keepdims=True) + EPS)
    gcol = jnp.stack([gam_col[rows(c), SM_A + h:SM_A + h + 1] for c, h in pairs])
    grow = jnp.stack([gam_row[SM_A + h:SM_A + h + 1, rows(c)] for c, h in pairs])
    bcol = jnp.stack([beta_all[rows(c), SM_B + h:SM_B + h + 1] for c, h in pairs])

    dec = jnp.exp(jnp.where(incl, gcol - grow, -jnp.inf))
    kkm = _mm_nt(k, k)
    a_pos = bcol * kkm * jnp.where(strict, dec, 0.0)
    tinv = eye - a_pos
    mp = -a_pos
    for _ in range(5):
        mp = _mm_nn(mp, mp)
        tinv = tinv + _mm_nn(tinv, mp)
    eg = jnp.exp(gcol)
    rhs = jnp.concatenate([bcol * v, (bcol * eg) * k], axis=-1)
    sol = _mm_nn(tinv, rhs)
    resid = rhs - sol - _mm_nn3(a_pos, sol)
    sol = sol + _mm_nn(tinv, resid)
    u_new, w = sol[..., :dv], sol[..., dv:]
    attn = _mm_nt(q, k) * dec
    q_dec = q * eg
    gend = gcol[:, CHUNK - 1:CHUNK, :]
    k_dec = k * jnp.exp(gend - gcol)
    g_end = jnp.exp(gend)

    s_cur = s_ref[...]
    for c in range(nc):
        hs = slice(c * nh, (c + 1) * nh)
        u = u_new[hs] - _mm_nn(w[hs], s_cur)
        o = _mm_nn(q_dec[hs], s_cur) + _mm_nn(attn[hs], u)
        s_cur = g_end[hs] * s_cur + _mm_tn(k_dec[hs], u)
        o = _rms(o, ng_ref[...][None])
        for h in range(nh):
            g = gate_ref[0, rows(c), h * dv:(h + 1) * dv].astype(F32)
            o_ref[0, rows(c), h * dv:(h + 1) * dv] = (o[h] * (g * _sigmoid(g))).astype(o_ref.dtype)
    s_ref[...] = s_cur


def _mixers_kernel(qkv_ref, sm_ref, gate_ref, cw_ref, alog_ref, dtb_ref, dng_ref,
                   gqk_ref, gv_ref, gr_ref, w2_ref, b2_ref, gng_ref, oa_ref, ob_ref,
                   ext_ref, s_ref, st_ref, *, tb):
    @pl.when(pl.program_id(1) == 0)
    def _():
        ext_ref[0:HALO, :] = jnp.zeros((HALO, ext_ref.shape[1]), F32)
        s_ref[...] = jnp.zeros(s_ref.shape, F32)
        st_ref[...] = jnp.zeros(st_ref.shape, F32)

    _deltanet_body(qkv_ref, sm_ref, gate_ref, cw_ref, alog_ref, dtb_ref, dng_ref, oa_ref, ext_ref, s_ref, tb=tb)
    _gla_body(gqk_ref, gv_ref, gr_ref, sm_ref, w2_ref, b2_ref, gng_ref, ob_ref, st_ref, tb=tb)


def _mixers(z3, sm3, conv_w, alog_row, dtb_row, dn_norm_g, w2p, b2, gla_norm_g, tb):
    b, l, _ = z3.shape
    w_qkv = 3 * DN_HEADS * DN_DK
    const = lambda bi, j: (0, 0)
    zblk = lambda off: pl.BlockSpec((1, tb, 512), lambda bi, j: (bi, j, off // 512))
    out_blk = pl.BlockSpec((1, tb, 512), lambda bi, j: (bi, j, 0))
    return pl.pallas_call(
        functools.partial(_mixers_kernel, tb=tb),
        grid=(b, l // tb),
        in_specs=[pl.BlockSpec((1, tb, w_qkv), lambda bi, j: (bi, j, Z_QKV // w_qkv)),
                  pl.BlockSpec((1, tb, LANES), lambda bi, j: (bi, j, 0)),
                  zblk(Z_GATE),
                  pl.BlockSpec((DN_CONV, w_qkv), const),
                  pl.BlockSpec((1, LANES), const),
                  pl.BlockSpec((1, LANES), const),
                  pl.BlockSpec((1, DN_DV), const),
                  zblk(Z_GQK), zblk(Z_GV), zblk(Z_GR),
                  pl.BlockSpec((LANES, GLA_HEADS * GLA_DK), const),
                  pl.BlockSpec((1, GLA_HEADS * GLA_DK), const),
                  pl.BlockSpec((1, GLA_DV), const)],
        out_specs=[out_blk, out_blk],
        out_shape=[jax.ShapeDtypeStruct((b, l, DN_HEADS * DN_DV), BF16),
                   jax.ShapeDtypeStruct((b, l, GLA_HEADS * GLA_DV), BF16)],
        scratch_shapes=[pltpu.VMEM((tb + HALO, w_qkv), F32),
                        pltpu.VMEM((DN_HEADS, DN_DK, DN_DV), F32),
                        pltpu.VMEM((GLA_HEADS, GLA_DV, GLA_DK), F32)],
        compiler_params=_cparams(("parallel", "arbitrary")),
        name="mixers",
    )(z3, sm3, z3, conv_w, alog_row, dtb_row, dn_norm_g, z3, z3, z3, w2p, b2, gla_norm_g)


def _gla_body(qk_ref, v_ref, r_ref, sm_ref, w2_ref, b2_ref, ng_ref, o_ref, st_ref, *, tb):
    nh, dk, dv = GLA_HEADS, GLA_DK, GLA_DV
    qk_w = nh * dk

    z = _mm(sm_ref[0], w2_ref[...]) + b2_ref[...]
    glog = -_softplus(-z) * (1.0 / GLA_NORMALIZER)
    bcum = _cumsum_rows(_chunk_tril(tb), glog)
    qk = qk_ref[0].astype(F32)
    q = qk[:, :qk_w] * (dk ** -0.5)
    k = qk[:, qk_w:]
    q_e = q * jnp.exp(bcum)
    k_e = k * jnp.exp(-bcum)

    ri = lax.broadcasted_iota(jnp.int32, (CHUNK, CHUNK), 0)
    ci = lax.broadcasted_iota(jnp.int32, (CHUNK, CHUNK), 1)
    incl = ci <= ri

    nc = tb // CHUNK
    pairs = [(c, h) for c in range(nc) for h in range(nh)]
    rows = lambda c: slice(c * CHUNK, (c + 1) * CHUNK)
    hsl = lambda h: slice(h * dk, (h + 1) * dk)
    b_end = [bcum[(c + 1) * CHUNK - 1:(c + 1) * CHUNK, :] for c in range(nc)]
    k_dec = [k[rows(c)] * jnp.exp(b_end[c] - bcum[rows(c)]) for c in range(nc)]
    qe_p = jnp.stack([q_e[rows(c), hsl(h)] for c, h in pairs])
    ke_p = jnp.stack([k_e[rows(c), hsl(h)] for c, h in pairs])
    kd_p = jnp.stack([k_dec[c][:, hsl(h)] for c, h in pairs])
    v_p = jnp.stack([v_ref[0, rows(c), h * dv:(h + 1) * dv] for c, h in pairs])
    ge_p = jnp.stack([jnp.exp(b_end[c][:, hsl(h)]) for c, h in pairs])
    attn = jnp.where(incl[None], _mm_nt(qe_p, ke_p), 0.0)
    intra = _mm_nn(attn, v_p)
    dst = _mm_tn(v_p, kd_p)

    st = st_ref[...]
    for c in range(nc):
        ps = slice(c * nh, (c + 1) * nh)
        o = intra[ps] + _mm_nt(qe_p[ps], st)
        st = ge_p[ps] * st + dst[ps]
        o = _rms(o, ng_ref[...][None])
        for h in range(nh):
            vs = slice(h * dv, (h + 1) * dv)
            g = r_ref[0, rows(c), vs].astype(F32)
            o_ref[0, rows(c), vs] = (o[h] * (g * _sigmoid(g))).astype(o_ref.dtype)
    st_ref[...] = st


def _s5_kernel(u_ref, p_ref, pw_ref, q_ref, r_ref, bb_ref, c_ref, ap_ref, y_ref,
               m_ref, taps_ref, ef_ref, es_ref, hp_ref, *, nbatch, ng):
    tc = S5_TC
    nrow = u_ref.shape[1]
    nchunk = nrow // nbatch

    def chunk_rows(g):
        return jnp.concatenate([u_ref[g * S5_GROUP + j] for j in range(S5_GROUP)], axis=-1)

    for g in range(ng):
        qr, qi = q_ref[g, :, 0:LANES], q_ref[g, :, LANES:2 * LANES]
        for j in range(S5_GROUP):
            br, bi = bb_ref[g, j:j + 1, 0:LANES], bb_ref[g, j:j + 1, LANES:2 * LANES]
            ef_ref[j * tc:(j + 1) * tc, 0:LANES] = (qr * br - qi * bi).astype(BF16)
            ef_ref[j * tc:(j + 1) * tc, LANES:2 * LANES] = (qr * bi + qi * br).astype(BF16)
        es_ref[g] = jnp.dot(chunk_rows(g), ef_ref[...], preferred_element_type=F32)

    ars = [ap_ref[g, 0:1, :] for g in range(ng)]
    ais = [ap_ref[g, 1:2, :] for g in range(ng)]

    def body(step, carry):
        new = []
        for g in range(ng):
            for b in range(nbatch):
                hr, hi = carry[2 * (g * nbatch + b)], carry[2 * (g * nbatch + b) + 1]
                base = pl.multiple_of(b * nchunk + step * HALO, HALO)
                et = es_ref[g, pl.ds(base, HALO), :]
                hrs, his = [], []
                for i in range(HALO):
                    hrs.append(hr)
                    his.append(hi)
                    er, ei = et[i:i + 1, 0:LANES], et[i:i + 1, LANES:2 * LANES]
                    hr, hi = ars[g] * hr - ais[g] * hi + er, ars[g] * hi + ais[g] * hr + ei
                hp_ref[g, pl.ds(base, HALO), 0:LANES] = jnp.concatenate(hrs, axis=0)
                hp_ref[g, pl.ds(base, HALO), LANES:2 * LANES] = jnp.concatenate(his, axis=0)
                new += [hr, hi]
        return tuple(new)

    zero = jnp.zeros((1, LANES), F32)
    lax.fori_loop(0, nchunk // HALO, body, (zero,) * (2 * nbatch * ng))

    causal = (lax.broadcasted_iota(jnp.int32, (tc, tc), 1) >= lax.broadcasted_iota(jnp.int32, (tc, tc), 0))
    for g in range(ng):
        taps_ref[...] = _mm_nn3(p_ref[g], pw_ref[g])

        def build(j, carry):
            kt = taps_ref[pl.ds(pl.multiple_of(j * S5_GROUP, S5_GROUP), S5_GROUP), :]
            for i in range(S5_GROUP):
                blk = pltpu.roll(jnp.broadcast_to(kt[i:i + 1, :], (tc, tc)), 0, 1, stride=1, stride_axis=0)
                m_ref[pl.ds(pl.multiple_of(j * tc, tc), tc), i * tc:(i + 1) * tc] = (
                    jnp.where(causal, blk, 0.0).astype(BF16))
            return carry

        lax.fori_loop(0, S5_GROUP, build, 0)
        y = jnp.dot(chunk_rows(g), m_ref[...], preferred_element_type=F32)
        rr, ri = r_ref[g, :, 0:LANES], r_ref[g, :, LANES:2 * LANES]
        for i in range(S5_GROUP):
            cr, ci = c_ref[g, i:i + 1, 0:LANES], c_ref[g, i:i + 1, LANES:2 * LANES]
            ef_ref[i * tc:(i + 1) * tc, 0:LANES] = (rr * cr - ri * ci).astype(BF16)
            ef_ref[i * tc:(i + 1) * tc, LANES:2 * LANES] = (-(rr * ci + ri * cr)).astype(BF16)
        y = y + _mm_nt(hp_ref[g], ef_ref[...])
        for i in range(S5_GROUP):
            y_ref[g * S5_GROUP + i] = y[:, i * tc:(i + 1) * tc].astype(y_ref.dtype)


def _s5(ut3, p_op, pw_op, q_op, r_op, bb_op, c_op, ap, layer, nbatch, ng):
    width, r, tc = ut3.shape
    g = width // S5_GROUP
    kw = S5_GROUP * tc
    kern = functools.partial(_s5_kernel, nbatch=nbatch, ng=ng)
    blk = lambda *shape: pl.BlockSpec((ng,) + shape, lambda i: (i + layer * (g // ng), 0, 0))
    chan = pl.BlockSpec((ng * S5_GROUP, r, tc), lambda i: (i, 0, 0))
    return pl.pallas_call(
        kern,
        grid=(g // ng,),
        in_specs=[chan, blk(S5_GROUP * S5_GROUP, 2 * S5_STATE), blk(2 * S5_STATE, tc),
                  blk(tc, 2 * LANES), blk(tc, 2 * LANES), blk(S5_GROUP, 2 * LANES), blk(S5_GROUP, 2 * LANES),
                  blk(HALO, LANES)],
        out_specs=chan,
        out_shape=jax.ShapeDtypeStruct((width, r, tc), BF16),
        scratch_shapes=[pltpu.VMEM((kw, kw), BF16),
                        pltpu.VMEM((S5_GROUP * S5_GROUP, tc), F32),
                        pltpu.VMEM((kw, 2 * LANES), BF16),
                        pltpu.VMEM((ng, r, 2 * LANES), F32),
                        pltpu.VMEM((ng, r, 2 * LANES), F32)],
        compiler_params=_cparams(("parallel",)),
        name="s5",
    )(ut3, p_op, pw_op, q_op, r_op, bb_op, c_op, ap)


def _s5_operators(lam_re, lam_im, log_step, b_re, b_im, c_re, c_im):
    tc, n, gs = S5_TC, S5_STATE, S5_GROUPS
    step = jnp.exp(log_step)[:, None]
    lr, li = lam_re, lam_im
    mag = jnp.exp(lr * step)
    abar_re, abar_im = mag * jnp.cos(li * step), mag * jnp.sin(li * step)
    den = lr * lr + li * li
    nr, ni = abar_re - 1.0, abar_im
    fr = (nr * lr + ni * li) / den
    fi = (ni * lr - nr * li) / den
    bb_re = fr[..., None] * b_re - fi[..., None] * b_im
    bb_im = fr[..., None] * b_im + fi[..., None] * b_re
    tau = jnp.arange(tc + 1, dtype=F32)[None, None, :]
    pmag = jnp.exp((lr * step)[..., None] * tau)
    ang = (li * step)[..., None] * tau
    p_re, p_im = pmag * jnp.cos(ang), pmag * jnp.sin(ang)
    bbt_re, bbt_im = bb_re.transpose(0, 2, 1)[:, :, None, :], bb_im.transpose(0, 2, 1)[:, :, None, :]
    cb_re = c_re[:, None] * bbt_re - c_im[:, None] * bbt_im
    cb_im = c_re[:, None] * bbt_im + c_im[:, None] * bbt_re
    p_op = jnp.concatenate([cb_re, -cb_im], axis=-1).reshape(gs, S5_GROUP * S5_GROUP, 2 * n)
    pw_op = jnp.concatenate([p_re[..., :tc], p_im[..., :tc]], axis=1)
    def lanes4(re, im):
        pad = jnp.zeros(re.shape[:-1] + (LANES - n,), F32)
        return jnp.concatenate([re, pad, im, pad], axis=-1)

    q_op = lanes4(p_re[..., :tc][..., ::-1].transpose(0, 2, 1), p_im[..., :tc][..., ::-1].transpose(0, 2, 1))
    r_op = lanes4(p_re[..., 1:].transpose(0, 2, 1), p_im[..., 1:].transpose(0, 2, 1))
    bb_op = lanes4(bb_re.transpose(0, 2, 1), bb_im.transpose(0, 2, 1))
    c_op = lanes4(c_re, c_im)
    ap = jnp.zeros((gs, HALO, LANES), F32)
    ap = ap.at[:, 0, :n].set(p_re[..., tc]).at[:, 1, :n].set(p_im[..., tc])
    return p_op, pw_op, q_op, r_op, bb_op, c_op, ap


def _merge_kernel(x_ref, ya_ref, yb_ref, ys_ref, u_ref, ng_ref, wg_ref, bg_ref, wb_ref, wo_ref,
                  d_ref, wglu_ref, bglu_ref, o_ref):
    x = x_ref[...]
    d = x.shape[1]
    h = _rms(x, ng_ref[...]).astype(BF16)
    yc = _gelu_tanh(ys_ref[...].astype(F32).T + d_ref[...] * u_ref[...].astype(F32))
    yc = yc * _sigmoid(_mm(yc, wglu_ref[...]) + bglu_ref[...])
    merged = None
    for i, y in enumerate((ya_ref[...], yb_ref[...], yc.astype(BF16))):
        gate = _sigmoid(jnp.dot(h, wg_ref[:, i * d:(i + 1) * d], preferred_element_type=F32)
                        + bg_ref[:, i * d:(i + 1) * d])
        term = gate * jnp.dot(y, wb_ref[i], preferred_element_type=F32)
        merged = term if merged is None else merged + term
    o_ref[...] = x + _mm(merged, wo_ref[...])


def _merge(x2, ya, yb, ys, z2, layer, ng, wg, bg, wb, wo, d, wglu, bglu, tm):
    t, dm = x2.shape
    bw = ya.shape[1]
    row = lambda w: pl.BlockSpec((tm, w), lambda i: (i, 0))
    const2 = lambda a: pl.BlockSpec(a.shape, lambda i: (0,) * a.ndim)
    lay = lambda a: _layer_spec(a, layer)
    return pl.pallas_call(
        _merge_kernel,
        grid=(t // tm,),
        in_specs=[row(dm), row(bw), row(bw),
                  pl.BlockSpec((S5_WIDTH, tm), lambda i: (0, i)),
                  pl.BlockSpec((tm, S5_WIDTH), lambda i: (i, Z_S5U // S5_WIDTH)),
                  const2(ng), lay(wg), const2(bg), lay(wb), lay(wo),
                  const2(d), lay(wglu), const2(bglu)],
        out_specs=row(dm),
        out_shape=jax.ShapeDtypeStruct((t, dm), F32),
        compiler_params=_cparams(("parallel",)),
        name="merge",
    )(x2, ya, yb, ys, z2, ng, wg, bg, wb, wo, d, wglu, bglu)


def _ffn_kernel(x_ref, p_ref, ng_ref, wu_ref, cw_ref, cb_ref, wd_ref, png_ref, wpg_ref,
                wpp_ref, fg_ref, o_ref, tail_ref, act_ref, *, tm, tiles_per_seq, ffc, down_group, final):
    dff = wd_ref.shape[0]

    @pl.when(pl.program_id(0) % tiles_per_seq == 0)
    def _():
        tail_ref[...] = jnp.zeros(tail_ref.shape, F32)

    x = x_ref[...]
    h = _rms(x, ng_ref[...]).astype(BF16)

    def up(c0):
        return (jnp.dot(h, wu_ref[:, c0:c0 + ffc], preferred_element_type=F32),
                jnp.dot(h, wu_ref[:, dff + c0:dff + c0 + ffc], preferred_element_type=F32))

    starts = list(range(0, dff, ffc))
    acc = None
    group0 = 0
    gu = up(0)
    for idx, c0 in enumerate(starts):
        cols = slice(c0, c0 + ffc)
        g, u = gu
        if idx + 1 < len(starts):
            gu = up(starts[idx + 1])
        ext = jnp.concatenate([tail_ref[:, cols], g], axis=0)
        tail_ref[:, cols] = g[tm - HALO:tm, :]
        cv = cw_ref[0:1, cols] * ext
        for kk in range(1, FFN_CONV):
            cv = pltpu.roll(cv, 1, 0) + cw_ref[kk:kk + 1, cols] * ext
        act_ref[:, cols] = (_gelu_tanh(cv[HALO:HALO + tm, :] + cb_ref[:, cols]) * u).astype(BF16)
        if (idx + 1) % down_group == 0 or idx + 1 == len(starts):
            grp = slice(group0, c0 + ffc)
            part = jnp.dot(act_ref[:, grp], wd_ref[grp, :], preferred_element_type=F32)
            acc = part if acc is None else acc + part
            group0 = c0 + ffc

    xn = x + acc
    gate = _sigmoid(_mm(_rms(xn, png_ref[...]), wpg_ref[...]))
    out = xn + gate * _mm(p_ref[0], wpp_ref[...])
    if final:
        out = _rms(out, fg_ref[...])
    o_ref[...] = out


def _ffn(x2, p3, layer, ng, wu, cw, cb, wd, png, wpg, wpp, fg, seq, tm, ffc, final):
    t, dm = x2.shape
    dff = wd.shape[1]
    assert dff % ffc == 0 and ffc % LANES == 0
    kern = functools.partial(_ffn_kernel, tm=tm, tiles_per_seq=seq // tm, ffc=ffc, down_group=6, final=final)
    const = lambda a: pl.BlockSpec(a.shape, lambda i: (0,) * a.ndim, pipeline_mode=pl.Buffered(1))
    lay = lambda a: _layer_spec(a, layer)
    return pl.pallas_call(
        kern,
        grid=(t // tm,),
        in_specs=[pl.BlockSpec((tm, dm), lambda i: (i, 0)),
                  pl.BlockSpec((1, tm, p3.shape[2]), lambda i: (layer, i, 0)),
                  const(ng), lay(wu), const(cw), const(cb), lay(wd),
                  const(png), lay(wpg), lay(wpp), const(fg)],
        out_specs=pl.BlockSpec((tm, dm), lambda i: (i, 0)),
        out_shape=jax.ShapeDtypeStruct((t, dm), F32),
        scratch_shapes=[pltpu.VMEM((HALO, dff), F32), pltpu.VMEM((tm, dff), BF16)],
        compiler_params=_cparams(("arbitrary",)),
        name="ffn",
    )(x2, p3, ng, wu, cw, cb, wd, png, wpg, wpp, fg)


def _reorder_w_in(w_in):
    qk, v = DN_HEADS * DN_DK, DN_HEADS * DN_DV
    gqk, gv = GLA_HEADS * GLA_DK, GLA_HEADS * GLA_DV
    o = 0
    seg = {}
    for name, width in (("dn_q", qk), ("dn_k", qk), ("dn_v", v), ("dn_b", DN_HEADS), ("dn_a", DN_HEADS),
                        ("dn_g", v), ("gl_q", gqk), ("gl_k", gqk), ("gl_v", gv), ("gl_lr", GLA_RANK),
                        ("gl_r", gv), ("s5_u", S5_WIDTH)):
        seg[name] = w_in[:, o:o + width]
        o += width
    d = w_in.shape[0]
    small = jnp.concatenate([seg["dn_b"], seg["dn_a"], seg["gl_lr"],
                             jnp.zeros((d, LANES - 2 * DN_HEADS - GLA_RANK), w_in.dtype)], axis=1)
    return jnp.concatenate([seg["dn_q"], seg["dn_k"], seg["dn_v"], seg["dn_g"], seg["gl_q"], seg["gl_k"],
                            seg["gl_v"], seg["gl_r"], seg["s5_u"], small], axis=1)


def _lane_row(vals, offset):
    return jnp.zeros((1, LANES), F32).at[0, offset:offset + vals.shape[0]].set(vals)


def kernel(x, p, attn_norm, w_in, dn_conv_w, dn_a_log, dn_dt_bias, dn_norm, gla_w2, gla_b2, gla_norm, s5_lam_re, s5_lam_im, s5_log_step, s5_b_re, s5_b_im, s5_c_re, s5_c_im, s5_d, s5_w_glu, s5_b_glu, w_gate, b_gate, w_branch, w_o, ffn_norm, w_up, ffn_conv_w, ffn_conv_b, w_down, ple_norm, w_ple_gate, w_ple_proj, final_norm):
    bsz, seq, dm = x.shape
    depth = w_in.shape[0]
    t = bsz * seq
    tb = min(512, seq)
    tm = min(512, seq)
    x2 = x.reshape(t, dm)
    p3 = p.reshape(depth, t, p.shape[-1])
    row = lambda a: a.reshape(1, -1)
    bf = lambda a: a.astype(BF16)
    w_in_r = bf(jnp.stack([_reorder_w_in(w_in[i]) for i in range(depth)]))
    w_gate_b, w_branch_b, w_o_b, w_glu_b = bf(w_gate), bf(w_branch), bf(w_o), bf(s5_w_glu)
    w_up_b, w_down_b, w_pg_b, w_pp_b = bf(w_up), bf(w_down), bf(w_ple_gate), bf(w_ple_proj)
    per_layer = [_s5_operators(s5_lam_re[i], s5_lam_im[i], s5_log_step[i], s5_b_re[i], s5_b_im[i],
                               s5_c_re[i], s5_c_im[i]) for i in range(depth)]
    s5_ops = [jnp.concatenate(tabs, axis=0) for tabs in zip(*per_layer)]
    for i in range(depth):
        z2, sm2, ut = _inproj(x2, row(attn_norm[i]), w_in_r, i, tm=tm, chunk=512)
        z3 = z2.reshape(bsz, seq, Z_WIDTH)
        sm3 = sm2.reshape(bsz, seq, LANES)
        w2p = jnp.zeros((LANES, GLA_HEADS * GLA_DK), F32).at[SM_LR:SM_LR + GLA_RANK].set(gla_w2[i])
        ya, yb = _mixers(z3, sm3, dn_conv_w[i], _lane_row(dn_a_log[i], SM_A), _lane_row(dn_dt_bias[i], SM_A),
                         row(dn_norm[i]), w2p.astype(BF16), row(gla_b2[i]), row(gla_norm[i]), tb)
        yt = _s5(ut.reshape(S5_WIDTH, t // S5_TC, S5_TC), *s5_ops, layer=i, nbatch=bsz, ng=2)
        x2 = _merge(x2, ya.reshape(t, -1), yb.reshape(t, -1), yt.reshape(S5_WIDTH, t), z2, i, row(attn_norm[i]),
                    w_gate_b, row(b_gate[i]), w_branch_b, w_o_b,
                    row(s5_d[i]), w_glu_b, row(s5_b_glu[i]), tm)
        x2 = _ffn(x2, p3, i, row(ffn_norm[i]), w_up_b, ffn_conv_w[i], row(ffn_conv_b[i]), w_down_b,
                  row(ple_norm[i]), w_pg_b, w_pp_b, row(final_norm), seq, tm, 256, i == depth - 1)
    return x2.reshape(bsz, seq, dm)
```

```python
import functools
import math

import jax
import jax.numpy as jnp
from jax import lax
from jax.experimental import pallas as pl
from jax.experimental.pallas import tpu as pltpu

F32 = jnp.float32
BF16 = jnp.bfloat16

EPS = 1e-6
CHUNK = 64
DN_HEADS, DN_DK, DN_DV, DN_CONV = 4, 128, 128, 4
GLA_HEADS, GLA_DK, GLA_DV, GLA_RANK = 4, 64, 128, 16
GLA_NORMALIZER = 16.0
S5_WIDTH, S5_GROUP, S5_STATE = 512, 16, 64
S5_GROUPS = S5_WIDTH // S5_GROUP
S5_TC = 64
FFN_CONV = 3
LANES = 128
HALO = 8

Z_QKV, Z_GATE, Z_GQK, Z_GV, Z_GR, Z_S5U, Z_SMALL = 0, 1536, 2048, 2560, 3072, 3584, 4096
Z_WIDTH = 4224
SM_B, SM_A, SM_LR = 0, 4, 8

VMEM_LIMIT = 56 * 1024 * 1024


def _cparams(sem):
    return pltpu.CompilerParams(dimension_semantics=sem, vmem_limit_bytes=VMEM_LIMIT)


def _layer_spec(a, layer):
    zeros = (0,) * (a.ndim - 1)
    return pl.BlockSpec((None,) + a.shape[1:], lambda *_: (layer,) + zeros, pipeline_mode=pl.Buffered(1))


def _mm(a, b):
    return jnp.dot(a.astype(BF16), b.astype(BF16), preferred_element_type=F32)


def _mm_nt(a, b):
    nb = a.ndim - 2
    batch = tuple(range(nb))
    return lax.dot_general(a.astype(BF16), b.astype(BF16),
                           (((a.ndim - 1,), (b.ndim - 1,)), (batch, batch)),
                           preferred_element_type=F32)


def _mm_nn(a, b):
    nb = a.ndim - 2
    batch = tuple(range(nb))
    return lax.dot_general(a.astype(BF16), b.astype(BF16),
                           (((a.ndim - 1,), (b.ndim - 2,)), (batch, batch)),
                           preferred_element_type=F32)


def _mm_tn(a, b):
    nb = a.ndim - 2
    batch = tuple(range(nb))
    return lax.dot_general(a.astype(BF16), b.astype(BF16),
                           (((a.ndim - 2,), (b.ndim - 2,)), (batch, batch)),
                           preferred_element_type=F32)


def _split3(a):
    a1 = a.astype(BF16)
    r1 = a - a1.astype(F32)
    a2 = r1.astype(BF16)
    a3 = (r1 - a2.astype(F32)).astype(BF16)
    return a1, a2, a3


def _cumsum_rows(lmat, a):
    a1, a2, a3 = _split3(a)
    d = lambda p: jnp.dot(lmat, p, preferred_element_type=F32)
    return d(a1) + d(a2) + d(a3)


def _mm_nn3(a, b):
    a1 = a.astype(BF16)
    a2 = (a - a1.astype(F32)).astype(BF16)
    b1 = b.astype(BF16)
    b2 = (b - b1.astype(F32)).astype(BF16)
    return _mm_nn(a1, b1) + _mm_nn(a1, b2) + _mm_nn(a2, b1)


def _rms(xf, g):
    ms = jnp.mean(xf * xf, axis=-1, keepdims=True)
    return xf * lax.rsqrt(ms + EPS) * g


def _sigmoid(x):
    return 0.5 * jnp.tanh(0.5 * x) + 0.5


def _softplus(x):
    return jnp.maximum(x, 0.0) + jnp.log(1.0 + jnp.exp(-jnp.abs(x)))


def _gelu_tanh(x):
    c = math.sqrt(2.0 / math.pi)
    return 0.5 * x * (1.0 + jnp.tanh(c * (x + 0.044715 * (x * x * x))))


def _chunk_tril(tb):
    r = lax.broadcasted_iota(jnp.int32, (tb, tb), 0)
    c = lax.broadcasted_iota(jnp.int32, (tb, tb), 1)
    same = (r // CHUNK) == (c // CHUNK)
    return jnp.where(same & (c <= r), 1.0, 0.0).astype(BF16)


def _inproj_kernel(x_ref, g_ref, w_ref, o_ref, sm_ref, ut_ref, *, chunk):
    h = _rms(x_ref[...], g_ref[...]).astype(BF16)
    n = w_ref.shape[1]
    tail = jnp.dot(h, w_ref[:, Z_S5U:n], preferred_element_type=F32)
    o_ref[:, Z_S5U:n] = tail.astype(o_ref.dtype)
    sm_ref[...] = tail[:, Z_SMALL - Z_S5U:Z_SMALL - Z_S5U + LANES]
    ut_ref[...] = tail[:, 0:S5_WIDTH].T.astype(BF16)
    for c0 in range(0, Z_S5U, chunk):
        o_ref[:, c0:c0 + chunk] = jnp.dot(h, w_ref[:, c0:c0 + chunk],
                                          preferred_element_type=F32).astype(o_ref.dtype)


def _inproj(x2, g, w, layer, tm, chunk):
    t, d = x2.shape
    n = w.shape[2]
    assert n == Z_WIDTH and Z_S5U % chunk == 0
    return pl.pallas_call(
        functools.partial(_inproj_kernel, chunk=chunk),
        grid=(t // tm,),
        in_specs=[pl.BlockSpec((tm, d), lambda i: (i, 0)),
                  pl.BlockSpec((1, d), lambda i: (0, 0)),
                  _layer_spec(w, layer)],
        out_specs=[pl.BlockSpec((tm, n), lambda i: (i, 0)),
                   pl.BlockSpec((tm, LANES), lambda i: (i, 0)),
                   pl.BlockSpec((S5_WIDTH, tm), lambda i: (0, i))],
        out_shape=[jax.ShapeDtypeStruct((t, n), BF16),
                   jax.ShapeDtypeStruct((t, LANES), F32),
                   jax.ShapeDtypeStruct((S5_WIDTH, t), BF16)],
        compiler_params=_cparams(("parallel",)),
        name="inproj",
    )(x2, g, w)


def _deltanet_body(qkv_ref, sm_ref, gate_ref, cw_ref, alog_ref, dtb_ref, ng_ref, o_ref,
                   ext_ref, s_ref, *, tb):
    nh, dk, dv = DN_HEADS, DN_DK, DN_DV
    qk_w = nh * dk

    ext_ref[HALO:HALO + tb, :] = qkv_ref[0].astype(F32)
    ext = ext_ref[...]
    acc = cw_ref[0:1, :] * ext
    for kk in range(1, DN_CONV):
        acc = pltpu.roll(acc, 1, 0) + cw_ref[kk:kk + 1, :] * ext
    ext_ref[0:HALO, :] = ext[tb:tb + HALO, :]
    acc = acc[HALO:HALO + tb, :]
    qkv = acc * _sigmoid(acc)

    sm = sm_ref[0]
    lane = lax.broadcasted_iota(jnp.int32, (1, LANES), 1)
    coef = jnp.where((lane >= SM_A) & (lane < SM_A + nh), -jnp.exp(alog_ref[...]), 0.0)
    beta_all = _sigmoid(sm)
    glog = coef * _softplus(sm + dtb_ref[...])
    ltri = _chunk_tril(tb)
    gam_col = _cumsum_rows(ltri, glog)
    gam_row = gam_col.T

    ri = lax.broadcasted_iota(jnp.int32, (CHUNK, CHUNK), 0)
    ci = lax.broadcasted_iota(jnp.int32, (CHUNK, CHUNK), 1)
    incl = (ci <= ri)[None]
    strict = (ci < ri)[None]
    eye = jnp.where(ci == ri, 1.0, 0.0)[None]

    nc = tb // CHUNK
    pairs = [(c, h) for c in range(nc) for h in range(nh)]
    rows = lambda c: slice(c * CHUNK, (c + 1) * CHUNK)
    q = jnp.stack([qkv[rows(c), h * dk:(h + 1) * dk] for c, h in pairs])
    k = jnp.stack([qkv[rows(c), qk_w + h * dk:qk_w + (h + 1) * dk] for c, h in pairs])
    v = jnp.stack([qkv[rows(c), 2 * qk_w + h * dv:2 * qk_w + (h + 1) * dv] for c, h in pairs])
    q = q * lax.rsqrt(jnp.sum(q * q, axis=-1, keepdims=True) + EPS) * (dk ** -0.5)
    k = k * lax.rsqrt(jnp.sum(k * k, axis=-1, keepdims=True) + EPS)
    gcol = jnp.stack([gam_col[rows(c), SM_A + h:SM_A + h + 1] for c, h in pairs])
    grow = jnp.stack([gam_row[SM_A + h:SM_A + h + 1, rows(c)] for c, h in pairs])
    bcol = jnp.stack([beta_all[rows(c), SM_B + h:SM_B + h + 1] for c, h in pairs])

    dec = jnp.exp(jnp.where(incl, gcol - grow, -jnp.inf))
    kkm = _mm_nt(k, k)
    a_pos = bcol * kkm * jnp.where(strict, dec, 0.0)
    tinv = eye - a_pos
    mp = -a_pos
    for _ in range(5):
        mp = _mm_nn(mp, mp)
        tinv = tinv + _mm_nn(tinv, mp)
    eg = jnp.exp(gcol)
    rhs = jnp.concatenate([bcol * v, (bcol * eg) * k], axis=-1)
    sol = _mm_nn(tinv, rhs)
    resid = rhs - sol - _mm_nn3(a_pos, sol)
    sol = sol + _mm_nn(tinv, resid)
    u_new, w = sol[..., :dv], sol[..., dv:]
    attn = _mm_nt(q, k) * dec
    q_dec = q * eg
    gend = gcol[:, CHUNK - 1:CHUNK, :]
    k_dec = k * jnp.exp(gend - gcol)
    g_end = jnp.exp(gend)

    s_cur = s_ref[...]
    for c in range(nc):
        hs = slice(c * nh, (c + 1) * nh)
        u = u_new[hs] - _mm_nn(w[hs], s_cur)
        o = _mm_nn(q_dec[hs], s_cur) + _mm_nn(attn[hs], u)
        s_cur = g_end[hs] * s_cur + _mm_tn(k_dec[hs], u)
        o = _rms(o, ng_ref[...][None])
        for h in range(nh):
            g = gate_ref[0, rows(c), h * dv:(h + 1) * dv].astype(F32)
            o_ref[0, rows(c), h * dv:(h + 1) * dv] = (o[h] * (g * _sigmoid(g))).astype(o_ref.dtype)
    s_ref[...] = s_cur


def _mixers_kernel(qkv_ref, sm_ref, gate_ref, cw_ref, alog_ref, dtb_ref, dng_ref,
                   gqk_ref, gv_ref, gr_ref, w2_ref, b2_ref, gng_ref, oa_ref, ob_ref,
                   ext_ref, s_ref, st_ref, *, tb):
    @pl.when(pl.program_id(1) == 0)
    def _():
        ext_ref[0:HALO, :] = jnp.zeros((HALO, ext_ref.shape[1]), F32)
        s_ref[...] = jnp.zeros(s_ref.shape, F32)
        st_ref[...] = jnp.zeros(st_ref.shape, F32)

    _deltanet_body(qkv_ref, sm_ref, gate_ref, cw_ref, alog_ref, dtb_ref, dng_ref, oa_ref, ext_ref, s_ref, tb=tb)
    _gla_body(gqk_ref, gv_ref, gr_ref, sm_ref, w2_ref, b2_ref, gng_ref, ob_ref, st_ref, tb=tb)


def _mixers(z3, sm3, conv_w, alog_row, dtb_row, dn_norm_g, w2p, b2, gla_norm_g, tb):
    b, l, _ = z3.shape
    w_qkv = 3 * DN_HEADS * DN_DK
    const = lambda bi, j: (0, 0)
    zblk = lambda off: pl.BlockSpec((1, tb, 512), lambda bi, j: (bi, j, off // 512))
    out_blk = pl.BlockSpec((1, tb, 512), lambda bi, j: (bi, j, 0))
    return pl.pallas_call(
        functools.partial(_mixers_kernel, tb=tb),
        grid=(b, l // tb),
        in_specs=[pl.BlockSpec((1, tb, w_qkv), lambda bi, j: (bi, j, Z_QKV // w_qkv)),
                  pl.BlockSpec((1, tb, LANES), lambda bi, j: (bi, j, 0)),
                  zblk(Z_GATE),
                  pl.BlockSpec((DN_CONV, w_qkv), const),
                  pl.BlockSpec((1, LANES), const),
                  pl.BlockSpec((1, LANES), const),
                  pl.BlockSpec((1, DN_DV), const),
                  zblk(Z_GQK), zblk(Z_GV), zblk(Z_GR),
                  pl.BlockSpec((LANES, GLA_HEADS * GLA_DK), const),
                  pl.BlockSpec((1, GLA_HEADS * GLA_DK), const),
                  pl.BlockSpec((1, GLA_DV), const)],
        out_specs=[out_blk, out_blk],
        out_shape=[jax.ShapeDtypeStruct((b, l, DN_HEADS * DN_DV), BF16),
                   jax.ShapeDtypeStruct((b, l, GLA_HEADS * GLA_DV), BF16)],
        scratch_shapes=[pltpu.VMEM((tb + HALO, w_qkv), F32),
                        pltpu.VMEM((DN_HEADS, DN_DK, DN_DV), F32),
                        pltpu.VMEM((GLA_HEADS, GLA_DV, GLA_DK), F32)],
        compiler_params=_cparams(("parallel", "arbitrary")),
        name="mixers",
    )(z3, sm3, z3, conv_w, alog_row, dtb_row, dn_norm_g, z3, z3, z3, w2p, b2, gla_norm_g)


def _gla_body(qk_ref, v_ref, r_ref, sm_ref, w2_ref, b2_ref, ng_ref, o_ref, st_ref, *, tb):
    nh, dk, dv = GLA_HEADS, GLA_DK, GLA_DV
    qk_w = nh * dk

    z = _mm(sm_ref[0], w2_ref[...]) + b2_ref[...]
    glog = -_softplus(-z) * (1.0 / GLA_NORMALIZER)
    bcum = _cumsum_rows(_chunk_tril(tb), glog)
    qk = qk_ref[0].astype(F32)
    q = qk[:, :qk_w] * (dk ** -0.5)
    k = qk[:, qk_w:]
    q_e = q * jnp.exp(bcum)
    k_e = k * jnp.exp(-bcum)

    ri = lax.broadcasted_iota(jnp.int32, (CHUNK, CHUNK), 0)
    ci = lax.broadcasted_iota(jnp.int32, (CHUNK, CHUNK), 1)
    incl = ci <= ri

    nc = tb // CHUNK
    pairs = [(c, h) for c in range(nc) for h in range(nh)]
    rows = lambda c: slice(c * CHUNK, (c + 1) * CHUNK)
    hsl = lambda h: slice(h * dk, (h + 1) * dk)
    b_end = [bcum[(c + 1) * CHUNK - 1:(c + 1) * CHUNK, :] for c in range(nc)]
    k_dec = [k[rows(c)] * jnp.exp(b_end[c] - bcum[rows(c)]) for c in range(nc)]
    qe_p = jnp.stack([q_e[rows(c), hsl(h)] for c, h in pairs])
    ke_p = jnp.stack([k_e[rows(c), hsl(h)] for c, h in pairs])
    kd_p = jnp.stack([k_dec[c][:, hsl(h)] for c, h in pairs])
    v_p = jnp.stack([v_ref[0, rows(c), h * dv:(h + 1) * dv] for c, h in pairs])
    ge_p = jnp.stack([jnp.exp(b_end[c][:, hsl(h)]) for c, h in pairs])
    attn = jnp.where(incl[None], _mm_nt(qe_p, ke_p), 0.0)
    intra = _mm_nn(attn, v_p)
    dst = _mm_tn(v_p, kd_p)

    st = st_ref[...]
    for c in range(nc):
        ps = slice(c * nh, (c + 1) * nh)
        o = intra[ps] + _mm_nt(qe_p[ps], st)
        st = ge_p[ps] * st + dst[ps]
        o = _rms(o, ng_ref[...][None])
        for h in range(nh):
            vs = slice(h * dv, (h + 1) * dv)
            g = r_ref[0, rows(c), vs].astype(F32)
            o_ref[0, rows(c), vs] = (o[h] * (g * _sigmoid(g))).astype(o_ref.dtype)
    st_ref[...] = st


def _s5_kernel(u_ref, p_ref, pw_ref, q_ref, r_ref, bb_ref, c_ref, ap_ref, y_ref,
               m_ref, taps_ref, ef_ref, es_ref, hp_ref, *, nbatch, ng):
    tc = S5_TC
    nrow = u_ref.shape[1]
    nchunk = nrow // nbatch

    def chunk_rows(g):
        return jnp.concatenate([u_ref[g * S5_GROUP + j] for j in range(S5_GROUP)], axis=-1)

    for g in range(ng):
        qr, qi = q_ref[g, :, 0:LANES], q_ref[g, :, LANES:2 * LANES]
        for j in range(S5_GROUP):
            br, bi = bb_ref[g, j:j + 1, 0:LANES], bb_ref[g, j:j + 1, LANES:2 * LANES]
            ef_ref[j * tc:(j + 1) * tc, 0:LANES] = (qr * br - qi * bi).astype(BF16)
            ef_ref[j * tc:(j + 1) * tc, LANES:2 * LANES] = (qr * bi + qi * br).astype(BF16)
        es_ref[g] = jnp.dot(chunk_rows(g), ef_ref[...], preferred_element_type=F32)

    ars = [ap_ref[g, 0:1, :] for g in range(ng)]
    ais = [ap_ref[g, 1:2, :] for g in range(ng)]

    def body(step, carry):
        new = []
        for g in range(ng):
            for b in range(nbatch):
                hr, hi = carry[2 * (g * nbatch + b)], carry[2 * (g * nbatch + b) + 1]
                base = pl.multiple_of(b * nchunk + step * HALO, HALO)
                et = es_ref[g, pl.ds(base, HALO), :]
                hrs, his = [], []
                for i in range(HALO):
                    hrs.append(hr)
                    his.append(hi)
                    er, ei = et[i:i + 1, 0:LANES], et[i:i + 1, LANES:2 * LANES]
                    hr, hi = ars[g] * hr - ais[g] * hi + er, ars[g] * hi + ais[g] * hr + ei
                hp_ref[g, pl.ds(base, HALO), 0:LANES] = jnp.concatenate(hrs, axis=0)
                hp_ref[g, pl.ds(base, HALO), LANES:2 * LANES] = jnp.concatenate(his, axis=0)
                new += [hr, hi]
        return tuple(new)

    zero = jnp.zeros((1, LANES), F32)
    lax.fori_loop(0, nchunk // HALO, body, (zero,) * (2 * nbatch * ng))

    pack = LANES // tc
    npair = S5_GROUP // pack
    causal = ((lax.broadcasted_iota(jnp.int32, (tc, LANES), 1) & (tc - 1))
              >= lax.broadcasted_iota(jnp.int32, (tc, LANES), 0))
    for g in range(ng):
        taps_ref[...] = _mm_nn3(p_ref[g], pw_ref[g])

        def build(j, carry):
            kt = taps_ref[pl.ds(pl.multiple_of(j * npair, npair), npair), :]
            for ip in range(npair):
                blk = pltpu.roll(jnp.broadcast_to(kt[ip:ip + 1, :], (tc, LANES)), 0, 1, stride=1, stride_axis=0)
                m_ref[pl.ds(pl.multiple_of(j * tc, tc), tc), ip * LANES:(ip + 1) * LANES] = (
                    jnp.where(causal, blk, 0.0).astype(BF16))
            return carry

        lax.fori_loop(0, S5_GROUP, build, 0)
        y = jnp.dot(chunk_rows(g), m_ref[...], preferred_element_type=F32)
        rr, ri = r_ref[g, :, 0:LANES], r_ref[g, :, LANES:2 * LANES]
        for i in range(S5_GROUP):
            cr, ci = c_ref[g, i:i + 1, 0:LANES], c_ref[g, i:i + 1, LANES:2 * LANES]
            ef_ref[i * tc:(i + 1) * tc, 0:LANES] = (rr * cr - ri * ci).astype(BF16)
            ef_ref[i * tc:(i + 1) * tc, LANES:2 * LANES] = (-(rr * ci + ri * cr)).astype(BF16)
        y = y + _mm_nt(hp_ref[g], ef_ref[...])
        for i in range(S5_GROUP):
            y_ref[g * S5_GROUP + i] = y[:, i * tc:(i + 1) * tc].astype(y_ref.dtype)


def _s5(ut3, p_op, pw_op, q_op, r_op, bb_op, c_op, ap, layer, nbatch, ng):
    width, r, tc = ut3.shape
    g = width // S5_GROUP
    kw = S5_GROUP * tc
    pack = LANES // tc
    tap_rows = S5_GROUP * S5_GROUP // pack
    kern = functools.partial(_s5_kernel, nbatch=nbatch, ng=ng)
    blk = lambda *shape: pl.BlockSpec((ng,) + shape, lambda i: (i + layer * (g // ng), 0, 0))
    chan = pl.BlockSpec((ng * S5_GROUP, r, tc), lambda i: (i, 0, 0))
    return pl.pallas_call(
        kern,
        grid=(g // ng,),
        in_specs=[chan, blk(tap_rows, pack * 2 * S5_STATE), blk(pack * 2 * S5_STATE, LANES),
                  blk(tc, 2 * LANES), blk(tc, 2 * LANES), blk(S5_GROUP, 2 * LANES), blk(S5_GROUP, 2 * LANES),
                  blk(HALO, LANES)],
        out_specs=chan,
        out_shape=jax.ShapeDtypeStruct((width, r, tc), BF16),
        scratch_shapes=[pltpu.VMEM((kw, kw), BF16),
                        pltpu.VMEM((tap_rows, LANES), F32),
                        pltpu.VMEM((kw, 2 * LANES), BF16),
                        pltpu.VMEM((ng, r, 2 * LANES), F32),
                        pltpu.VMEM((ng, r, 2 * LANES), F32)],
        compiler_params=_cparams(("parallel",)),
        name="s5",
    )(ut3, p_op, pw_op, q_op, r_op, bb_op, c_op, ap)


def _s5_operators(lam_re, lam_im, log_step, b_re, b_im, c_re, c_im):
    tc, n, gs = S5_TC, S5_STATE, S5_GROUPS
    step = jnp.exp(log_step)[:, None]
    lr, li = lam_re, lam_im
    mag = jnp.exp(lr * step)
    abar_re, abar_im = mag * jnp.cos(li * step), mag * jnp.sin(li * step)
    den = lr * lr + li * li
    nr, ni = abar_re - 1.0, abar_im
    fr = (nr * lr + ni * li) / den
    fi = (ni * lr - nr * li) / den
    bb_re = fr[..., None] * b_re - fi[..., None] * b_im
    bb_im = fr[..., None] * b_im + fi[..., None] * b_re
    tau = jnp.arange(tc + 1, dtype=F32)[None, None, :]
    pmag = jnp.exp((lr * step)[..., None] * tau)
    ang = (li * step)[..., None] * tau
    p_re, p_im = pmag * jnp.cos(ang), pmag * jnp.sin(ang)
    bbt_re, bbt_im = bb_re.transpose(0, 2, 1)[:, :, None, :], bb_im.transpose(0, 2, 1)[:, :, None, :]
    cb_re = c_re[:, None] * bbt_re - c_im[:, None] * bbt_im
    cb_im = c_re[:, None] * bbt_im + c_im[:, None] * bbt_re
    pack = LANES // tc
    p_op = jnp.concatenate([cb_re, -cb_im], axis=-1).reshape(gs, S5_GROUP * S5_GROUP // pack, pack * 2 * n)
    pw1 = jnp.concatenate([p_re[..., :tc], p_im[..., :tc]], axis=1)
    zero = jnp.zeros_like(pw1)
    pw_op = jnp.concatenate([jnp.concatenate([pw1 if a == b else zero for b in range(pack)], axis=2)
                             for a in range(pack)], axis=1)
    def lanes4(re, im):
        pad = jnp.zeros(re.shape[:-1] + (LANES - n,), F32)
        return jnp.concatenate([re, pad, im, pad], axis=-1)

    q_op = lanes4(p_re[..., :tc][..., ::-1].transpose(0, 2, 1), p_im[..., :tc][..., ::-1].transpose(0, 2, 1))
    r_op = lanes4(p_re[..., 1:].transpose(0, 2, 1), p_im[..., 1:].transpose(0, 2, 1))
    bb_op = lanes4(bb_re.transpose(0, 2, 1), bb_im.transpose(0, 2, 1))
    c_op = lanes4(c_re, c_im)
    ap = jnp.zeros((gs, HALO, LANES), F32)
    ap = ap.at[:, 0, :n].set(p_re[..., tc]).at[:, 1, :n].set(p_im[..., tc])
    return p_op, pw_op, q_op, r_op, bb_op, c_op, ap


def _merge_kernel(x_ref, ya_ref, yb_ref, ys_ref, u_ref, ng_ref, wg_ref, bg_ref, wb_ref, wo_ref,
                  d_ref, wglu_ref, bglu_ref, o_ref):
    x = x_ref[...]
    d = x.shape[1]
    h = _rms(x, ng_ref[...]).astype(BF16)
    yc = _gelu_tanh(ys_ref[...].astype(F32).T + d_ref[...] * u_ref[...].astype(F32))
    yc = yc * _sigmoid(_mm(yc, wglu_ref[...]) + bglu_ref[...])
    merged = None
    for i, y in enumerate((ya_ref[...], yb_ref[...], yc.astype(BF16))):
        gate = _sigmoid(jnp.dot(h, wg_ref[:, i * d:(i + 1) * d], preferred_element_type=F32)
                        + bg_ref[:, i * d:(i + 1) * d])
        term = gate * jnp.dot(y, wb_ref[i], preferred_element_type=F32)
        merged = term if merged is None else merged + term
    o_ref[...] = x + _mm(merged, wo_ref[...])


def _merge(x2, ya, yb, ys, z2, layer, ng, wg, bg, wb, wo, d, wglu, bglu, tm):
    t, dm = x2.shape
    bw = ya.shape[1]
    row = lambda w: pl.BlockSpec((tm, w), lambda i: (i, 0))
    const2 = lambda a: pl.BlockSpec(a.shape, lambda i: (0,) * a.ndim)
    lay = lambda a: _layer_spec(a, layer)
    return pl.pallas_call(
        _merge_kernel,
        grid=(t // tm,),
        in_specs=[row(dm), row(bw), row(bw),
                  pl.BlockSpec((S5_WIDTH, tm), lambda i: (0, i)),
                  pl.BlockSpec((tm, S5_WIDTH), lambda i: (i, Z_S5U // S5_WIDTH)),
                  const2(ng), lay(wg), const2(bg), lay(wb), lay(wo),
                  const2(d), lay(wglu), const2(bglu)],
        out_specs=row(dm),
        out_shape=jax.ShapeDtypeStruct((t, dm), F32),
        compiler_params=_cparams(("parallel",)),
        name="merge",
    )(x2, ya, yb, ys, z2, ng, wg, bg, wb, wo, d, wglu, bglu)


def _ffn_kernel(x_ref, p_ref, ng_ref, wu_ref, cw_ref, cb_ref, wd_ref, png_ref, wpg_ref,
                wpp_ref, fg_ref, o_ref, tail_ref, act_ref, *, tm, tiles_per_seq, ffc, down_group, final):
    dff = wd_ref.shape[0]

    @pl.when(pl.program_id(0) % tiles_per_seq == 0)
    def _():
        tail_ref[...] = jnp.zeros(tail_ref.shape, F32)

    x = x_ref[...]
    h = _rms(x, ng_ref[...]).astype(BF16)

    def up(c0):
        return (jnp.dot(h, wu_ref[:, c0:c0 + ffc], preferred_element_type=F32),
                jnp.dot(h, wu_ref[:, dff + c0:dff + c0 + ffc], preferred_element_type=F32))

    starts = list(range(0, dff, ffc))
    acc = None
    group0 = 0
    gu = up(0)
    for idx, c0 in enumerate(starts):
        cols = slice(c0, c0 + ffc)
        g, u = gu
        if idx + 1 < len(starts):
            gu = up(starts[idx + 1])
        ext = jnp.concatenate([tail_ref[:, cols], g], axis=0)
        tail_ref[:, cols] = g[tm - HALO:tm, :]
        cv = cw_ref[0:1, cols] * ext
        for kk in range(1, FFN_CONV):
            cv = pltpu.roll(cv, 1, 0) + cw_ref[kk:kk + 1, cols] * ext
        act_ref[:, cols] = (_gelu_tanh(cv[HALO:HALO + tm, :] + cb_ref[:, cols]) * u).astype(BF16)
        if (idx + 1) % down_group == 0 or idx + 1 == len(starts):
            grp = slice(group0, c0 + ffc)
            part = jnp.dot(act_ref[:, grp], wd_ref[grp, :], preferred_element_type=F32)
            acc = part if acc is None else acc + part
            group0 = c0 + ffc

    xn = x + acc
    gate = _sigmoid(_mm(_rms(xn, png_ref[...]), wpg_ref[...]))
    out = xn + gate * _mm(p_ref[0], wpp_ref[...])
    if final:
        out = _rms(out, fg_ref[...])
    o_ref[...] = out


def _ffn(x2, p3, layer, ng, wu, cw, cb, wd, png, wpg, wpp, fg, seq, tm, ffc, final):
    t, dm = x2.shape
    dff = wd.shape[1]
    assert dff % ffc == 0 and ffc % LANES == 0
    kern = functools.partial(_ffn_kernel, tm=tm, tiles_per_seq=seq // tm, ffc=ffc, down_group=6, final=final)
    const = lambda a: pl.BlockSpec(a.shape, lambda i: (0,) * a.ndim, pipeline_mode=pl.Buffered(1))
    lay = lambda a: _layer_spec(a, layer)
    return pl.pallas_call(
        kern,
        grid=(t // tm,),
        in_specs=[pl.BlockSpec((tm, dm), lambda i: (i, 0)),
                  pl.BlockSpec((1, tm, p3.shape[2]), lambda i: (layer, i, 0)),
                  const(ng), lay(wu), const(cw), const(cb), lay(wd),
                  const(png), lay(wpg), lay(wpp), const(fg)],
        out_specs=pl.BlockSpec((tm, dm), lambda i: (i, 0)),
        out_shape=jax.ShapeDtypeStruct((t, dm), F32),
        scratch_shapes=[pltpu.VMEM((HALO, dff), F32), pltpu.VMEM((tm, dff), BF16)],
        compiler_params=_cparams(("arbitrary",)),
        name="ffn",
    )(x2, p3, ng, wu, cw, cb, wd, png, wpg, wpp, fg)


def _reorder_w_in(w_in):
    qk, v = DN_HEADS * DN_DK, DN_HEADS * DN_DV
    gqk, gv = GLA_HEADS * GLA_DK, GLA_HEADS * GLA_DV
    o = 0
    seg = {}
    for name, width in (("dn_q", qk), ("dn_k", qk), ("dn_v", v), ("dn_b", DN_HEADS), ("dn_a", DN_HEADS),
                        ("dn_g", v), ("gl_q", gqk), ("gl_k", gqk), ("gl_v", gv), ("gl_lr", GLA_RANK),
                        ("gl_r", gv), ("s5_u", S5_WIDTH)):
        seg[name] = w_in[:, o:o + width]
        o += width
    d = w_in.shape[0]
    small = jnp.concatenate([seg["dn_b"], seg["dn_a"], seg["gl_lr"],
                             jnp.zeros((d, LANES - 2 * DN_HEADS - GLA_RANK), w_in.dtype)], axis=1)
    return jnp.concatenate([seg["dn_q"], seg["dn_k"], seg["dn_v"], seg["dn_g"], seg["gl_q"], seg["gl_k"],
                            seg["gl_v"], seg["gl_r"], seg["s5_u"], small], axis=1)


def _lane_row(vals, offset):
    return jnp.zeros((1, LANES), F32).at[0, offset:offset + vals.shape[0]].set(vals)


def kernel(x, p, attn_norm, w_in, dn_conv_w, dn_a_log, dn_dt_bias, dn_norm, gla_w2, gla_b2, gla_norm, s5_lam_re, s5_lam_im, s5_log_step, s5_b_re, s5_b_im, s5_c_re, s5_c_im, s5_d, s5_w_glu, s5_b_glu, w_gate, b_gate, w_branch, w_o, ffn_norm, w_up, ffn_conv_w, ffn_conv_b, w_down, ple_norm, w_ple_gate, w_ple_proj, final_norm):
    bsz, seq, dm = x.shape
    depth = w_in.shape[0]
    t = bsz * seq
    tb = min(512, seq)
    tm = min(512, seq)
    x2 = x.reshape(t, dm)
    p3 = p.reshape(depth, t, p.shape[-1])
    row = lambda a: a.reshape(1, -1)
    bf = lambda a: a.astype(BF16)
    w_in_r = bf(jnp.stack([_reorder_w_in(w_in[i]) for i in range(depth)]))
    w_gate_b, w_branch_b, w_o_b, w_glu_b = bf(w_gate), bf(w_branch), bf(w_o), bf(s5_w_glu)
    w_up_b, w_down_b, w_pg_b, w_pp_b = bf(w_up), bf(w_down), bf(w_ple_gate), bf(w_ple_proj)
    per_layer = [_s5_operators(s5_lam_re[i], s5_lam_im[i], s5_log_step[i], s5_b_re[i], s5_b_im[i],
                               s5_c_re[i], s5_c_im[i]) for i in range(depth)]
    s5_ops = [jnp.concatenate(tabs, axis=0) for tabs in zip(*per_layer)]
    for i in range(depth):
        z2, sm2, ut = _inproj(x2, row(attn_norm[i]), w_in_r, i, tm=tm, chunk=512)
        z3 = z2.reshape(bsz, seq, Z_WIDTH)
        sm3 = sm2.reshape(bsz, seq, LANES)
        w2p = jnp.zeros((LANES, GLA_HEADS * GLA_DK), F32).at[SM_LR:SM_LR + GLA_RANK].set(gla_w2[i])
        ya, yb = _mixers(z3, sm3, dn_conv_w[i], _lane_row(dn_a_log[i], SM_A), _lane_row(dn_dt_bias[i], SM_A),
                         row(dn_norm[i]), w2p.astype(BF16), row(gla_b2[i]), row(gla_norm[i]), tb)
        yt = _s5(ut.reshape(S5_WIDTH, t // S5_TC, S5_TC), *s5_ops, layer=i, nbatch=bsz, ng=2)
        x2 = _merge(x2, ya.reshape(t, -1), yb.reshape(t, -1), yt.reshape(S5_WIDTH, t), z2, i, row(attn_norm[i]),
                    w_gate_b, row(b_gate[i]), w_branch_b, w_o_b,
                    row(s5_d[i]), w_glu_b, row(s5_b_glu[i]), tm)
        x2 = _ffn(x2, p3, i, row(ffn_norm[i]), w_up_b, ffn_conv_w[i], row(ffn_conv_b[i]), w_down_b,
                  row(ple_norm[i]), w_pg_b, w_pp_b, row(final_norm), seq, tm, 256, i == depth - 1)
    return x2.reshape(bsz, seq, dm)
```

```python
import functools
import math

import jax
import jax.numpy as jnp
from jax import lax
from jax.experimental import pallas as pl
from jax.experimental.pallas import tpu as pltpu

F32 = jnp.float32
BF16 = jnp.bfloat16

EPS = 1e-6
CHUNK = 64
DN_HEADS, DN_DK, DN_DV, DN_CONV = 4, 128, 128, 4
GLA_HEADS, GLA_DK, GLA_DV, GLA_RANK = 4, 64, 128, 16
GLA_NORMALIZER = 16.0
S5_WIDTH, S5_GROUP, S5_STATE = 512, 16, 64
S5_GROUPS = S5_WIDTH // S5_GROUP
S5_TC = 64
FFN_CONV = 3
LANES = 128
HALO = 8

Z_QKV, Z_GATE, Z_GQK, Z_GV, Z_GR, Z_S5U, Z_SMALL = 0, 1536, 2048, 2560, 3072, 3584, 4096
Z_WIDTH = 4224
SM_B, SM_A, SM_LR = 0, 4, 8

VMEM_LIMIT = 56 * 1024 * 1024


def _cparams(sem):
    return pltpu.CompilerParams(dimension_semantics=sem, vmem_limit_bytes=VMEM_LIMIT)


def _layer_spec(a, layer):
    zeros = (0,) * (a.ndim - 1)
    return pl.BlockSpec((None,) + a.shape[1:], lambda *_: (layer,) + zeros, pipeline_mode=pl.Buffered(1))


def _mm(a, b):
    return jnp.dot(a.astype(BF16), b.astype(BF16), preferred_element_type=F32)


def _mm_nt(a, b):
    nb = a.ndim - 2
    batch = tuple(range(nb))
    return lax.dot_general(a.astype(BF16), b.astype(BF16),
                           (((a.ndim - 1,), (b.ndim - 1,)), (batch, batch)),
                           preferred_element_type=F32)


def _mm_nn(a, b):
    nb = a.ndim - 2
    batch = tuple(range(nb))
    return lax.dot_general(a.astype(BF16), b.astype(BF16),
                           (((a.ndim - 1,), (b.ndim - 2,)), (batch, batch)),
                           preferred_element_type=F32)


def _mm_tn(a, b):
    nb = a.ndim - 2
    batch = tuple(range(nb))
    return lax.dot_general(a.astype(BF16), b.astype(BF16),
                           (((a.ndim - 2,), (b.ndim - 2,)), (batch, batch)),
                           preferred_element_type=F32)


def _split3(a):
    a1 = a.astype(BF16)
    r1 = a - a1.astype(F32)
    a2 = r1.astype(BF16)
    a3 = (r1 - a2.astype(F32)).astype(BF16)
    return a1, a2, a3


def _cumsum_rows(lmat, a):
    a1, a2, a3 = _split3(a)
    d = lambda p: jnp.dot(lmat, p, preferred_element_type=F32)
    return d(a1) + d(a2) + d(a3)


def _mm_nn3(a, b):
    a1 = a.astype(BF16)
    a2 = (a - a1.astype(F32)).astype(BF16)
    b1 = b.astype(BF16)
    b2 = (b - b1.astype(F32)).astype(BF16)
    return _mm_nn(a1, b1) + _mm_nn(a1, b2) + _mm_nn(a2, b1)


def _rms(xf, g):
    ms = jnp.mean(xf * xf, axis=-1, keepdims=True)
    return xf * lax.rsqrt(ms + EPS) * g


def _sigmoid(x):
    return 0.5 * jnp.tanh(0.5 * x) + 0.5


def _softplus(x):
    return jnp.maximum(x, 0.0) + jnp.log(1.0 + jnp.exp(-jnp.abs(x)))


def _gelu_tanh(x):
    c = math.sqrt(2.0 / math.pi)
    return 0.5 * x * (1.0 + jnp.tanh(c * (x + 0.044715 * (x * x * x))))


def _chunk_tril(tb):
    r = lax.broadcasted_iota(jnp.int32, (tb, tb), 0)
    c = lax.broadcasted_iota(jnp.int32, (tb, tb), 1)
    same = (r // CHUNK) == (c // CHUNK)
    return jnp.where(same & (c <= r), 1.0, 0.0).astype(BF16)


def _inproj_kernel(x_ref, g_ref, w_ref, o_ref, sm_ref, ut_ref, *, chunk):
    h = _rms(x_ref[...], g_ref[...]).astype(BF16)
    n = w_ref.shape[1]
    tail = jnp.dot(h, w_ref[:, Z_S5U:n], preferred_element_type=F32)
    o_ref[:, Z_S5U:n] = tail.astype(o_ref.dtype)
    sm_ref[...] = tail[:, Z_SMALL - Z_S5U:Z_SMALL - Z_S5U + LANES]
    ut_ref[...] = tail[:, 0:S5_WIDTH].T.astype(BF16)
    for c0 in range(0, Z_S5U, chunk):
        o_ref[:, c0:c0 + chunk] = jnp.dot(h, w_ref[:, c0:c0 + chunk],
                                          preferred_element_type=F32).astype(o_ref.dtype)


def _inproj(x2, g, w, layer, tm, chunk):
    t, d = x2.shape
    n = w.shape[2]
    assert n == Z_WIDTH and Z_S5U % chunk == 0
    return pl.pallas_call(
        functools.partial(_inproj_kernel, chunk=chunk),
        grid=(t // tm,),
        in_specs=[pl.BlockSpec((tm, d), lambda i: (i, 0)),
                  pl.BlockSpec((1, d), lambda i: (0, 0)),
                  _layer_spec(w, layer)],
        out_specs=[pl.BlockSpec((tm, n), lambda i: (i, 0)),
                   pl.BlockSpec((tm, LANES), lambda i: (i, 0)),
                   pl.BlockSpec((S5_WIDTH, tm), lambda i: (0, i))],
        out_shape=[jax.ShapeDtypeStruct((t, n), BF16),
                   jax.ShapeDtypeStruct((t, LANES), F32),
                   jax.ShapeDtypeStruct((S5_WIDTH, t), BF16)],
        compiler_params=_cparams(("parallel",)),
        name="inproj",
    )(x2, g, w)


def _deltanet_body(qkv_ref, sm_ref, gate_ref, cw_ref, alog_ref, dtb_ref, ng_ref, o_ref,
                   ext_ref, s_ref, *, tb):
    nh, dk, dv = DN_HEADS, DN_DK, DN_DV
    qk_w = nh * dk

    ext_ref[HALO:HALO + tb, :] = qkv_ref[0].astype(F32)
    ext = ext_ref[...]
    acc = cw_ref[0:1, :] * ext
    for kk in range(1, DN_CONV):
        acc = pltpu.roll(acc, 1, 0) + cw_ref[kk:kk + 1, :] * ext
    ext_ref[0:HALO, :] = ext[tb:tb + HALO, :]
    acc = acc[HALO:HALO + tb, :]
    qkv = acc * _sigmoid(acc)

    sm = sm_ref[0]
    lane = lax.broadcasted_iota(jnp.int32, (1, LANES), 1)
    coef = jnp.where((lane >= SM_A) & (lane < SM_A + nh), -jnp.exp(alog_ref[...]), 0.0)
    beta_all = _sigmoid(sm)
    glog = coef * _softplus(sm + dtb_ref[...])
    ltri = _chunk_tril(tb)
    gam_col = _cumsum_rows(ltri, glog)
    gam_row = gam_col.T

    ri = lax.broadcasted_iota(jnp.int32, (CHUNK, CHUNK), 0)
    ci = lax.broadcasted_iota(jnp.int32, (CHUNK, CHUNK), 1)
    incl = (ci <= ri)[None]
    strict = (ci < ri)[None]
    eye = jnp.where(ci == ri, 1.0, 0.0)[None]

    nc = tb // CHUNK
    pairs = [(c, h) for c in range(nc) for h in range(nh)]
    rows = lambda c: slice(c * CHUNK, (c + 1) * CHUNK)
    q = jnp.stack([qkv[rows(c), h * dk:(h + 1) * dk] for c, h in pairs])
    k = jnp.stack([qkv[rows(c), qk_w + h * dk:qk_w + (h + 1) * dk] for c, h in pairs])
    v = jnp.stack([qkv[rows(c), 2 * qk_w + h * dv:2 * qk_w + (h + 1) * dv] for c, h in pairs])
    q = q * lax.rsqrt(jnp.sum(q * q, axis=-1, keepdims=True) + EPS) * (dk ** -0.5)
    k = k * lax.rsqrt(jnp.sum(k * k, axis=-1, keepdims=True) + EPS)
    gcol = jnp.stack([gam_col[rows(c), SM_A + h:SM_A + h + 1] for c, h in pairs])
    grow = jnp.stack([gam_row[SM_A + h:SM_A + h + 1, rows(c)] for c, h in pairs])
    bcol = jnp.stack([beta_all[rows(c), SM_B + h:SM_B + h + 1] for c, h in pairs])

    dec = jnp.exp(jnp.where(incl, gcol - grow, -jnp.inf))
    kkm = _mm_nt(k, k)
    a_pos = bcol * kkm * jnp.where(strict, dec, 0.0)
    tinv = eye - a_pos
    mp = -a_pos
    for _ in range(5):
        mp = _mm_nn(mp, mp)
        tinv = tinv + _mm_nn(tinv, mp)
    eg = jnp.exp(gcol)
    rhs = jnp.concatenate([bcol * v, (bcol * eg) * k], axis=-1)
    sol = _mm_nn(tinv, rhs)
    resid = rhs - sol - _mm_nn3(a_pos, sol)
    sol = sol + _mm_nn(tinv, resid)
    u_new, w = sol[..., :dv], sol[..., dv:]
    attn = _mm_nt(q, k) * dec
    q_dec = q * eg
    gend = gcol[:, CHUNK - 1:CHUNK, :]
    k_dec = k * jnp.exp(gend - gcol)
    g_end = jnp.exp(gend)

    s_cur = s_ref[...]
    for c in range(nc):
        hs = slice(c * nh, (c + 1) * nh)
        u = u_new[hs] - _mm_nn(w[hs], s_cur)
        o = _mm_nn(q_dec[hs], s_cur) + _mm_nn(attn[hs], u)
        s_cur = g_end[hs] * s_cur + _mm_tn(k_dec[hs], u)
        o = _rms(o, ng_ref[...][None])
        for h in range(nh):
            g = gate_ref[0, rows(c), h * dv:(h + 1) * dv].astype(F32)
            o_ref[0, rows(c), h * dv:(h + 1) * dv] = (o[h] * (g * _sigmoid(g))).astype(o_ref.dtype)
    s_ref[...] = s_cur


def _mixers_kernel(qkv_ref, sm_ref, gate_ref, cw_ref, alog_ref, dtb_ref, dng_ref,
                   gqk_ref, gv_ref, gr_ref, w2_ref, b2_ref, gng_ref, oa_ref, ob_ref,
                   ext_ref, s_ref, st_ref, *, tb):
    @pl.when(pl.program_id(1) == 0)
    def _():
        ext_ref[0:HALO, :] = jnp.zeros((HALO, ext_ref.shape[1]), F32)
        s_ref[...] = jnp.zeros(s_ref.shape, F32)
        st_ref[...] = jnp.zeros(st_ref.shape, F32)

    _deltanet_body(qkv_ref, sm_ref, gate_ref, cw_ref, alog_ref, dtb_ref, dng_ref, oa_ref, ext_ref, s_ref, tb=tb)
    _gla_body(gqk_ref, gv_ref, gr_ref, sm_ref, w2_ref, b2_ref, gng_ref, ob_ref, st_ref, tb=tb)


def _mixers(z3, sm3, conv_w, alog_row, dtb_row, dn_norm_g, w2p, b2, gla_norm_g, tb):
    b, l, _ = z3.shape
    w_qkv = 3 * DN_HEADS * DN_DK
    const = lambda bi, j: (0, 0)
    zblk = lambda off: pl.BlockSpec((1, tb, 512), lambda bi, j: (bi, j, off // 512))
    out_blk = pl.BlockSpec((1, tb, 512), lambda bi, j: (bi, j, 0))
    return pl.pallas_call(
        functools.partial(_mixers_kernel, tb=tb),
        grid=(b, l // tb),
        in_specs=[pl.BlockSpec((1, tb, w_qkv), lambda bi, j: (bi, j, Z_QKV // w_qkv)),
                  pl.BlockSpec((1, tb, LANES), lambda bi, j: (bi, j, 0)),
                  zblk(Z_GATE),
                  pl.BlockSpec((DN_CONV, w_qkv), const),
                  pl.BlockSpec((1, LANES), const),
                  pl.BlockSpec((1, LANES), const),
                  pl.BlockSpec((1, DN_DV), const),
                  zblk(Z_GQK), zblk(Z_GV), zblk(Z_GR),
                  pl.BlockSpec((LANES, GLA_HEADS * GLA_DK), const),
                  pl.BlockSpec((1, GLA_HEADS * GLA_DK), const),
                  pl.BlockSpec((1, GLA_DV), const)],
        out_specs=[out_blk, out_blk],
        out_shape=[jax.ShapeDtypeStruct((b, l, DN_HEADS * DN_DV), BF16),
                   jax.ShapeDtypeStruct((b, l, GLA_HEADS * GLA_DV), BF16)],
        scratch_shapes=[pltpu.VMEM((tb + HALO, w_qkv), F32),
                        pltpu.VMEM((DN_HEADS, DN_DK, DN_DV), F32),
                        pltpu.VMEM((GLA_HEADS, GLA_DV, GLA_DK), F32)],
        compiler_params=_cparams(("parallel", "arbitrary")),
        name="mixers",
    )(z3, sm3, z3, conv_w, alog_row, dtb_row, dn_norm_g, z3, z3, z3, w2p, b2, gla_norm_g)


def _gla_body(qk_ref, v_ref, r_ref, sm_ref, w2_ref, b2_ref, ng_ref, o_ref, st_ref, *, tb):
    nh, dk, dv = GLA_HEADS, GLA_DK, GLA_DV
    qk_w = nh * dk

    z = _mm(sm_ref[0], w2_ref[...]) + b2_ref[...]
    glog = -_softplus(-z) * (1.0 / GLA_NORMALIZER)
    bcum = _cumsum_rows(_chunk_tril(tb), glog)
    qk = qk_ref[0].astype(F32)
    q = qk[:, :qk_w] * (dk ** -0.5)
    k = qk[:, qk_w:]
    q_e = q * jnp.exp(bcum)
    k_e = k * jnp.exp(-bcum)

    ri = lax.broadcasted_iota(jnp.int32, (CHUNK, CHUNK), 0)
    ci = lax.broadcasted_iota(jnp.int32, (CHUNK, CHUNK), 1)
    incl = ci <= ri

    nc = tb // CHUNK
    pairs = [(c, h) for c in range(nc) for h in range(nh)]
    rows = lambda c: slice(c * CHUNK, (c + 1) * CHUNK)
    hsl = lambda h: slice(h * dk, (h + 1) * dk)
    b_end = [bcum[(c + 1) * CHUNK - 1:(c + 1) * CHUNK, :] for c in range(nc)]
    k_dec = [k[rows(c)] * jnp.exp(b_end[c] - bcum[rows(c)]) for c in range(nc)]
    qe_p = jnp.stack([q_e[rows(c), hsl(h)] for c, h in pairs])
    ke_p = jnp.stack([k_e[rows(c), hsl(h)] for c, h in pairs])
    kd_p = jnp.stack([k_dec[c][:, hsl(h)] for c, h in pairs])
    v_p = jnp.stack([v_ref[0, rows(c), h * dv:(h + 1) * dv] for c, h in pairs])
    ge_p = jnp.stack([jnp.exp(b_end[c][:, hsl(h)]) for c, h in pairs])
    attn = jnp.where(incl[None], _mm_nt(qe_p, ke_p), 0.0)
    intra = _mm_nn(attn, v_p)
    dst = _mm_tn(v_p, kd_p)

    st = st_ref[...]
    for c in range(nc):
        ps = slice(c * nh, (c + 1) * nh)
        o = intra[ps] + _mm_nt(qe_p[ps], st)
        st = ge_p[ps] * st + dst[ps]
        o = _rms(o, ng_ref[...][None])
        for h in range(nh):
            vs = slice(h * dv, (h + 1) * dv)
            g = r_ref[0, rows(c), vs].astype(F32)
            o_ref[0, rows(c), vs] = (o[h] * (g * _sigmoid(g))).astype(o_ref.dtype)
    st_ref[...] = st


def _s5_kernel(u_ref, p_ref, pw_ref, q_ref, r_ref, bb_ref, c_ref, ap_ref, y_ref,
               m_ref, taps_ref, ef_ref, es_ref, hp_ref, *, nbatch, ng):
    tc = S5_TC
    pack = LANES // tc
    nrow = u_ref.shape[1]
    rows_per_batch = nrow // nbatch

    def chunk_rows(g):
        us = [u_ref[g * S5_GROUP + j].astype(F32) for j in range(S5_GROUP)]
        return jnp.concatenate(
            [jnp.concatenate([uj[:, h * tc:(h + 1) * tc] for uj in us], axis=-1) for h in range(pack)],
            axis=0).astype(BF16)

    for g in range(ng):
        qr, qi = q_ref[g, :, 0:LANES], q_ref[g, :, LANES:2 * LANES]
        for j in range(S5_GROUP):
            br, bi = bb_ref[g, j:j + 1, 0:LANES], bb_ref[g, j:j + 1, LANES:2 * LANES]
            ef_ref[j * tc:(j + 1) * tc, 0:LANES] = (qr * br - qi * bi).astype(BF16)
            ef_ref[j * tc:(j + 1) * tc, LANES:2 * LANES] = (qr * bi + qi * br).astype(BF16)
        es_ref[g] = jnp.dot(chunk_rows(g), ef_ref[...], preferred_element_type=F32)

    ars = [ap_ref[g, 0:1, :] for g in range(ng)]
    ais = [ap_ref[g, 1:2, :] for g in range(ng)]

    def body(step, carry):
        new = []
        for g in range(ng):
            for b in range(nbatch):
                hr, hi = carry[2 * (g * nbatch + b)], carry[2 * (g * nbatch + b) + 1]
                bases = [pl.multiple_of(h * nrow + b * rows_per_batch + step * HALO, HALO) for h in range(pack)]
                ets = [es_ref[g, pl.ds(base, HALO), :] for base in bases]
                hrs, his = [[] for _ in range(pack)], [[] for _ in range(pack)]
                for i in range(HALO):
                    for h in range(pack):
                        hrs[h].append(hr)
                        his[h].append(hi)
                        er, ei = ets[h][i:i + 1, 0:LANES], ets[h][i:i + 1, LANES:2 * LANES]
                        hr, hi = ars[g] * hr - ais[g] * hi + er, ars[g] * hi + ais[g] * hr + ei
                for h in range(pack):
                    hp_ref[g, pl.ds(bases[h], HALO), 0:LANES] = jnp.concatenate(hrs[h], axis=0)
                    hp_ref[g, pl.ds(bases[h], HALO), LANES:2 * LANES] = jnp.concatenate(his[h], axis=0)
                new += [hr, hi]
        return tuple(new)

    zero = jnp.zeros((1, LANES), F32)
    lax.fori_loop(0, rows_per_batch // HALO, body, (zero,) * (2 * nbatch * ng))

    npair = S5_GROUP // pack
    causal = ((lax.broadcasted_iota(jnp.int32, (tc, LANES), 1) & (tc - 1))
              >= lax.broadcasted_iota(jnp.int32, (tc, LANES), 0))
    for g in range(ng):
        taps_ref[...] = _mm_nn3(p_ref[g], pw_ref[g])

        def build(j, carry):
            kt = taps_ref[pl.ds(pl.multiple_of(j * npair, npair), npair), :]
            for ip in range(npair):
                blk = pltpu.roll(jnp.broadcast_to(kt[ip:ip + 1, :], (tc, LANES)), 0, 1, stride=1, stride_axis=0)
                m_ref[pl.ds(pl.multiple_of(j * tc, tc), tc), ip * LANES:(ip + 1) * LANES] = (
                    jnp.where(causal, blk, 0.0).astype(BF16))
            return carry

        lax.fori_loop(0, S5_GROUP, build, 0)
        y = jnp.dot(chunk_rows(g), m_ref[...], preferred_element_type=F32)
        rr, ri = r_ref[g, :, 0:LANES], r_ref[g, :, LANES:2 * LANES]
        for i in range(S5_GROUP):
            cr, ci = c_ref[g, i:i + 1, 0:LANES], c_ref[g, i:i + 1, LANES:2 * LANES]
            ef_ref[i * tc:(i + 1) * tc, 0:LANES] = (rr * cr - ri * ci).astype(BF16)
            ef_ref[i * tc:(i + 1) * tc, LANES:2 * LANES] = (-(rr * ci + ri * cr)).astype(BF16)
        y = y + _mm_nt(hp_ref[g], ef_ref[...])
        for i in range(S5_GROUP):
            y_ref[g * S5_GROUP + i] = jnp.concatenate(
                [y[h * nrow:(h + 1) * nrow, i * tc:(i + 1) * tc] for h in range(pack)], axis=-1).astype(y_ref.dtype)


def _s5(ut3, p_op, pw_op, q_op, r_op, bb_op, c_op, ap, layer, nbatch, ng):
    width, r, lanes = ut3.shape
    assert lanes == LANES
    tc = S5_TC
    g = width // S5_GROUP
    kw = S5_GROUP * tc
    pack = LANES // tc
    tap_rows = S5_GROUP * S5_GROUP // pack
    kern = functools.partial(_s5_kernel, nbatch=nbatch, ng=ng)
    blk = lambda *shape: pl.BlockSpec((ng,) + shape, lambda i: (i + layer * (g // ng), 0, 0))
    chan = pl.BlockSpec((ng * S5_GROUP, r, LANES), lambda i: (i, 0, 0))
    return pl.pallas_call(
        kern,
        grid=(g // ng,),
        in_specs=[chan, blk(tap_rows, pack * 2 * S5_STATE), blk(pack * 2 * S5_STATE, LANES),
                  blk(tc, 2 * LANES), blk(tc, 2 * LANES), blk(S5_GROUP, 2 * LANES), blk(S5_GROUP, 2 * LANES),
                  blk(HALO, LANES)],
        out_specs=chan,
        out_shape=jax.ShapeDtypeStruct((width, r, LANES), BF16),
        scratch_shapes=[pltpu.VMEM((kw, kw), BF16),
                        pltpu.VMEM((tap_rows, LANES), F32),
                        pltpu.VMEM((kw, 2 * LANES), BF16),
                        pltpu.VMEM((ng, pack * r, 2 * LANES), F32),
                        pltpu.VMEM((ng, pack * r, 2 * LANES), F32)],
        compiler_params=_cparams(("parallel",)),
        name="s5",
    )(ut3, p_op, pw_op, q_op, r_op, bb_op, c_op, ap)


def _s5_operators(lam_re, lam_im, log_step, b_re, b_im, c_re, c_im):
    tc, n, gs = S5_TC, S5_STATE, S5_GROUPS
    step = jnp.exp(log_step)[:, None]
    lr, li = lam_re, lam_im
    mag = jnp.exp(lr * step)
    abar_re, abar_im = mag * jnp.cos(li * step), mag * jnp.sin(li * step)
    den = lr * lr + li * li
    nr, ni = abar_re - 1.0, abar_im
    fr = (nr * lr + ni * li) / den
    fi = (ni * lr - nr * li) / den
    bb_re = fr[..., None] * b_re - fi[..., None] * b_im
    bb_im = fr[..., None] * b_im + fi[..., None] * b_re
    tau = jnp.arange(tc + 1, dtype=F32)[None, None, :]
    pmag = jnp.exp((lr * step)[..., None] * tau)
    ang = (li * step)[..., None] * tau
    p_re, p_im = pmag * jnp.cos(ang), pmag * jnp.sin(ang)
    bbt_re, bbt_im = bb_re.transpose(0, 2, 1)[:, :, None, :], bb_im.transpose(0, 2, 1)[:, :, None, :]
    cb_re = c_re[:, None] * bbt_re - c_im[:, None] * bbt_im
    cb_im = c_re[:, None] * bbt_im + c_im[:, None] * bbt_re
    pack = LANES // tc
    p_op = jnp.concatenate([cb_re, -cb_im], axis=-1).reshape(gs, S5_GROUP * S5_GROUP // pack, pack * 2 * n)
    pw1 = jnp.concatenate([p_re[..., :tc], p_im[..., :tc]], axis=1)
    zero = jnp.zeros_like(pw1)
    pw_op = jnp.concatenate([jnp.concatenate([pw1 if a == b else zero for b in range(pack)], axis=2)
                             for a in range(pack)], axis=1)
    def lanes4(re, im):
        pad = jnp.zeros(re.shape[:-1] + (LANES - n,), F32)
        return jnp.concatenate([re, pad, im, pad], axis=-1)

    q_op = lanes4(p_re[..., :tc][..., ::-1].transpose(0, 2, 1), p_im[..., :tc][..., ::-1].transpose(0, 2, 1))
    r_op = lanes4(p_re[..., 1:].transpose(0, 2, 1), p_im[..., 1:].transpose(0, 2, 1))
    bb_op = lanes4(bb_re.transpose(0, 2, 1), bb_im.transpose(0, 2, 1))
    c_op = lanes4(c_re, c_im)
    ap = jnp.zeros((gs, HALO, LANES), F32)
    ap = ap.at[:, 0, :n].set(p_re[..., tc]).at[:, 1, :n].set(p_im[..., tc])
    return p_op, pw_op, q_op, r_op, bb_op, c_op, ap


def _merge_kernel(x_ref, ya_ref, yb_ref, ys_ref, u_ref, ng_ref, wg_ref, bg_ref, wb_ref, wo_ref,
                  d_ref, wglu_ref, bglu_ref, o_ref):
    x = x_ref[...]
    d = x.shape[1]
    h = _rms(x, ng_ref[...]).astype(BF16)
    yc = _gelu_tanh(ys_ref[...].astype(F32).T + d_ref[...] * u_ref[...].astype(F32))
    yc = yc * _sigmoid(_mm(yc, wglu_ref[...]) + bglu_ref[...])
    merged = None
    for i, y in enumerate((ya_ref[...], yb_ref[...], yc.astype(BF16))):
        gate = _sigmoid(jnp.dot(h, wg_ref[:, i * d:(i + 1) * d], preferred_element_type=F32)
                        + bg_ref[:, i * d:(i + 1) * d])
        term = gate * jnp.dot(y, wb_ref[i], preferred_element_type=F32)
        merged = term if merged is None else merged + term
    o_ref[...] = x + _mm(merged, wo_ref[...])


def _merge(x2, ya, yb, ys, z2, layer, ng, wg, bg, wb, wo, d, wglu, bglu, tm):
    t, dm = x2.shape
    bw = ya.shape[1]
    row = lambda w: pl.BlockSpec((tm, w), lambda i: (i, 0))
    const2 = lambda a: pl.BlockSpec(a.shape, lambda i: (0,) * a.ndim)
    lay = lambda a: _layer_spec(a, layer)
    return pl.pallas_call(
        _merge_kernel,
        grid=(t // tm,),
        in_specs=[row(dm), row(bw), row(bw),
                  pl.BlockSpec((S5_WIDTH, tm), lambda i: (0, i)),
                  pl.BlockSpec((tm, S5_WIDTH), lambda i: (i, Z_S5U // S5_WIDTH)),
                  const2(ng), lay(wg), const2(bg), lay(wb), lay(wo),
                  const2(d), lay(wglu), const2(bglu)],
        out_specs=row(dm),
        out_shape=jax.ShapeDtypeStruct((t, dm), F32),
        compiler_params=_cparams(("parallel",)),
        name="merge",
    )(x2, ya, yb, ys, z2, ng, wg, bg, wb, wo, d, wglu, bglu)


def _ffn_kernel(x_ref, p_ref, ng_ref, wu_ref, cw_ref, cb_ref, wd_ref, png_ref, wpg_ref,
                wpp_ref, fg_ref, o_ref, tail_ref, act_ref, *, tm, tiles_per_seq, ffc, down_group, final):
    dff = wd_ref.shape[0]

    @pl.when(pl.program_id(0) % tiles_per_seq == 0)
    def _():
        tail_ref[...] = jnp.zeros(tail_ref.shape, F32)

    x = x_ref[...]
    h = _rms(x, ng_ref[...]).astype(BF16)

    def up(c0):
        return (jnp.dot(h, wu_ref[:, c0:c0 + ffc], preferred_element_type=F32),
                jnp.dot(h, wu_ref[:, dff + c0:dff + c0 + ffc], preferred_element_type=F32))

    starts = list(range(0, dff, ffc))
    acc = None
    group0 = 0
    gu = up(0)
    for idx, c0 in enumerate(starts):
        cols = slice(c0, c0 + ffc)
        g, u = gu
        if idx + 1 < len(starts):
            gu = up(starts[idx + 1])
        ext = jnp.concatenate([tail_ref[:, cols], g], axis=0)
        tail_ref[:, cols] = g[tm - HALO:tm, :]
        cv = cw_ref[0:1, cols] * ext
        for kk in range(1, FFN_CONV):
            cv = pltpu.roll(cv, 1, 0) + cw_ref[kk:kk + 1, cols] * ext
        act_ref[:, cols] = (_gelu_tanh(cv[HALO:HALO + tm, :] + cb_ref[:, cols]) * u).astype(BF16)
        if (idx + 1) % down_group == 0 or idx + 1 == len(starts):
            grp = slice(group0, c0 + ffc)
            part = jnp.dot(act_ref[:, grp], wd_ref[grp, :], preferred_element_type=F32)
            acc = part if acc is None else acc + part
            group0 = c0 + ffc

    xn = x + acc
    gate = _sigmoid(_mm(_rms(xn, png_ref[...]), wpg_ref[...]))
    out = xn + gate * _mm(p_ref[0], wpp_ref[...])
    if final:
        out = _rms(out, fg_ref[...])
    o_ref[...] = out


def _ffn(x2, p3, layer, ng, wu, cw, cb, wd, png, wpg, wpp, fg, seq, tm, ffc, final):
    t, dm = x2.shape
    dff = wd.shape[1]
    assert dff % ffc == 0 and ffc % LANES == 0
    kern = functools.partial(_ffn_kernel, tm=tm, tiles_per_seq=seq // tm, ffc=ffc, down_group=6, final=final)
    const = lambda a: pl.BlockSpec(a.shape, lambda i: (0,) * a.ndim, pipeline_mode=pl.Buffered(1))
    lay = lambda a: _layer_spec(a, layer)
    return pl.pallas_call(
        kern,
        grid=(t // tm,),
        in_specs=[pl.BlockSpec((tm, dm), lambda i: (i, 0)),
                  pl.BlockSpec((1, tm, p3.shape[2]), lambda i: (layer, i, 0)),
                  const(ng), lay(wu), const(cw), const(cb), lay(wd),
                  const(png), lay(wpg), lay(wpp), const(fg)],
        out_specs=pl.BlockSpec((tm, dm), lambda i: (i, 0)),
        out_shape=jax.ShapeDtypeStruct((t, dm), F32),
        scratch_shapes=[pltpu.VMEM((HALO, dff), F32), pltpu.VMEM((tm, dff), BF16)],
        compiler_params=_cparams(("arbitrary",)),
        name="ffn",
    )(x2, p3, ng, wu, cw, cb, wd, png, wpg, wpp, fg)


def _reorder_w_in(w_in):
    qk, v = DN_HEADS * DN_DK, DN_HEADS * DN_DV
    gqk, gv = GLA_HEADS * GLA_DK, GLA_HEADS * GLA_DV
    o = 0
    seg = {}
    for name, width in (("dn_q", qk), ("dn_k", qk), ("dn_v", v), ("dn_b", DN_HEADS), ("dn_a", DN_HEADS),
                        ("dn_g", v), ("gl_q", gqk), ("gl_k", gqk), ("gl_v", gv), ("gl_lr", GLA_RANK),
                        ("gl_r", gv), ("s5_u", S5_WIDTH)):
        seg[name] = w_in[:, o:o + width]
        o += width
    d = w_in.shape[0]
    small = jnp.concatenate([seg["dn_b"], seg["dn_a"], seg["gl_lr"],
                             jnp.zeros((d, LANES - 2 * DN_HEADS - GLA_RANK), w_in.dtype)], axis=1)
    return jnp.concatenate([seg["dn_q"], seg["dn_k"], seg["dn_v"], seg["dn_g"], seg["gl_q"], seg["gl_k"],
                            seg["gl_v"], seg["gl_r"], seg["s5_u"], small], axis=1)


def _lane_row(vals, offset):
    return jnp.zeros((1, LANES), F32).at[0, offset:offset + vals.shape[0]].set(vals)


def kernel(x, p, attn_norm, w_in, dn_conv_w, dn_a_log, dn_dt_bias, dn_norm, gla_w2, gla_b2, gla_norm, s5_lam_re, s5_lam_im, s5_log_step, s5_b_re, s5_b_im, s5_c_re, s5_c_im, s5_d, s5_w_glu, s5_b_glu, w_gate, b_gate, w_branch, w_o, ffn_norm, w_up, ffn_conv_w, ffn_conv_b, w_down, ple_norm, w_ple_gate, w_ple_proj, final_norm):
    bsz, seq, dm = x.shape
    depth = w_in.shape[0]
    t = bsz * seq
    tb = min(512, seq)
    tm = min(512, seq)
    x2 = x.reshape(t, dm)
    p3 = p.reshape(depth, t, p.shape[-1])
    row = lambda a: a.reshape(1, -1)
    bf = lambda a: a.astype(BF16)
    w_in_r = bf(jnp.stack([_reorder_w_in(w_in[i]) for i in range(depth)]))
    w_gate_b, w_branch_b, w_o_b, w_glu_b = bf(w_gate), bf(w_branch), bf(w_o), bf(s5_w_glu)
    w_up_b, w_down_b, w_pg_b, w_pp_b = bf(w_up), bf(w_down), bf(w_ple_gate), bf(w_ple_proj)
    per_layer = [_s5_operators(s5_lam_re[i], s5_lam_im[i], s5_log_step[i], s5_b_re[i], s5_b_im[i],
                               s5_c_re[i], s5_c_im[i]) for i in range(depth)]
    s5_ops = [jnp.concatenate(tabs, axis=0) for tabs in zip(*per_layer)]
    for i in range(depth):
        z2, sm2, ut = _inproj(x2, row(attn_norm[i]), w_in_r, i, tm=tm, chunk=512)
        z3 = z2.reshape(bsz, seq, Z_WIDTH)
        sm3 = sm2.reshape(bsz, seq, LANES)
        w2p = jnp.zeros((LANES, GLA_HEADS * GLA_DK), F32).at[SM_LR:SM_LR + GLA_RANK].set(gla_w2[i])
        ya, yb = _mixers(z3, sm3, dn_conv_w[i], _lane_row(dn_a_log[i], SM_A), _lane_row(dn_dt_bias[i], SM_A),
                         row(dn_norm[i]), w2p.astype(BF16), row(gla_b2[i]), row(gla_norm[i]), tb)
        yt = _s5(ut.reshape(S5_WIDTH, t // LANES, LANES), *s5_ops, layer=i, nbatch=bsz, ng=2)
        x2 = _merge(x2, ya.reshape(t, -1), yb.reshape(t, -1), yt.reshape(S5_WIDTH, t), z2, i, row(attn_norm[i]),
                    w_gate_b, row(b_gate[i]), w_branch_b, w_o_b,
                    row(s5_d[i]), w_glu_b, row(s5_b_glu[i]), tm)
        x2 = _ffn(x2, p3, i, row(ffn_norm[i]), w_up_b, ffn_conv_w[i], row(ffn_conv_b[i]), w_down_b,
                  row(ple_norm[i]), w_pg_b, w_pp_b, row(final_norm), seq, tm, 256, i == depth - 1)
    return x2.reshape(bsz, seq, dm)
```
